```python
import jax
import jax.numpy as jnp
from jax import lax
import numpy as np

D_MODEL = 1024
BATCH = 8
SEQ = 2048
DEPTH = 4

N_MIXERS = 3
RMS_EPS = 1e-6

ATTN_HEAD_DIM = 64
ATTN_Q_HEADS = 16
ATTN_KV_HEADS = 4
ATTN_GROUP = ATTN_Q_HEADS // ATTN_KV_HEADS
WINDOW = 128
BLOCK = 128

POOL_WINDOWS = (2, 4, 8, 16)
POOL_GROUPS = 4
POOL_GROUP_DIM = D_MODEL // POOL_GROUPS

RWKV_HEAD_DIM = 64
RWKV_HEADS = D_MODEL // RWKV_HEAD_DIM
DECAY_LORA = 64
ICLR_LORA = 64
GATE_LORA = 128
GN_EPS = 64e-5

FFN_HIDDEN = ((8 * D_MODEL + 3 * 256 - 1) // (3 * 256)) * 256

kernel_name = 'hybrid_swa_pool_rwkv7_trunk'


def _n_layers_of(kind):
    return len(range(kind, DEPTH, N_MIXERS))


def _rmsnorm(x, g):
    xf = x.astype(jnp.float32)
    y = xf * lax.rsqrt(jnp.mean(xf * xf, axis=-1, keepdims=True) + RMS_EPS)
    return (y * g.astype(jnp.float32)).astype(x.dtype)


def _swa_sink_attention(xn, w_qkv, w_o, sinks):
    b, t, _ = xn.shape
    nb = t // BLOCK
    qd = ATTN_Q_HEADS * ATTN_HEAD_DIM
    kvd = ATTN_KV_HEADS * ATTN_HEAD_DIM
    qkv = xn @ w_qkv
    q = qkv[..., :qd].reshape(b, nb, BLOCK, ATTN_KV_HEADS, ATTN_GROUP, ATTN_HEAD_DIM)
    k = qkv[..., qd:qd + kvd].reshape(b, nb, BLOCK, ATTN_KV_HEADS, ATTN_HEAD_DIM)
    v = qkv[..., qd + kvd:].reshape(b, nb, BLOCK, ATTN_KV_HEADS, ATTN_HEAD_DIM)

    def band(z):
        prev = jnp.pad(z, ((0, 0), (1, 0), (0, 0), (0, 0), (0, 0)))[:, :nb]
        return jnp.concatenate([prev, z], axis=2)

    kb, vb = band(k), band(v)
    scale = ATTN_HEAD_DIM ** -0.5
    s = jnp.einsum('bnqhgd,bnshd->bnhgqs', q, kb).astype(jnp.float32) * scale
    qpos = jnp.arange(BLOCK)[:, None] + BLOCK
    kpos = jnp.arange(2 * BLOCK)[None, :]
    diff = qpos - kpos
    in_window = (diff >= 0) & (diff < WINDOW)
    blk = jnp.arange(nb)[:, None, None]
    valid = in_window[None] & ((blk > 0) | (kpos[None] >= BLOCK))
    s = jnp.where(valid[None, :, None, None], s, -jnp.inf)
    sink = sinks.astype(jnp.float32).reshape(1, 1, ATTN_KV_HEADS, ATTN_GROUP, 1, 1)
    m = jnp.maximum(jnp.max(s, axis=-1, keepdims=True), sink)
    p = jnp.exp(s - m)
    p = p / (jnp.sum(p, axis=-1, keepdims=True) + jnp.exp(sink - m))
    o = jnp.einsum('bnhgqs,bnshd->bnqhgd', p.astype(vb.dtype), vb)
    return o.reshape(b, t, qd) @ w_o


def _multiscale_pool(xn, pool_w, pool_scale):
    b, t, dm = xn.shape
    xf = xn.astype(jnp.float32)
    pos = jnp.arange(t)
    outs = []
    for gi, w in enumerate(POOL_WINDOWS):
        xg = xf[..., gi * POOL_GROUP_DIM:(gi + 1) * POOL_GROUP_DIM]
        cs = jnp.cumsum(xg, axis=1)
        lag = jnp.pad(cs, ((0, 0), (w, 0), (0, 0)))[:, :t]
        cnt = jnp.minimum(pos + 1, w).astype(jnp.float32)[None, :, None]
        outs.append((cs - lag) / cnt - xg)
    dpool = jnp.stack(outs, axis=2).astype(xn.dtype)
    y = jnp.einsum('btgc,gce->btge', dpool, pool_w).reshape(b, t, dm)
    return y * pool_scale


def _token_shift(x):
    return jnp.pad(x, ((0, 0), (1, 0), (0, 0)))[:, :-1]


def _rwkv7_time_mix(xn, mu, w_rkv, w0, w1, w2, a0, a1, a2, g1, g2, k_k, k_a, r_k, ln_w, ln_b, w_o):
    b, t, dm = xn.shape
    h, hd = RWKV_HEADS, RWKV_HEAD_DIM
    f32 = jnp.float32
    dx = _token_shift(xn) - xn
    xs = xn[None] + dx[None] * mu[:, None, None, :]
    rkv = jnp.einsum('cbtd,cde->cbte', xs[:3], w_rkv)
    r, k, v = rkv[0], rkv[1], rkv[2]
    xw, xa, xg = xs[3], xs[4], xs[5]
    w_log = -jax.nn.softplus(-(w0 + jnp.tanh(xw @ w1) @ w2)) - 0.5
    decay = jnp.exp(-jnp.exp(w_log.astype(f32)))
    a = jax.nn.sigmoid(a0 + (xa @ a1) @ a2)
    gate = jax.nn.sigmoid(xg @ g1) @ g2
    kk = (k * k_k).astype(f32).reshape(b, t, h, hd)
    kk = kk / jnp.maximum(jnp.sqrt(jnp.sum(kk * kk, axis=-1, keepdims=True)), 1e-12)
    kk = kk.reshape(b, t, dm)
    k = k * (1 + (a - 1) * k_a)

    def to_tm(z):
        return z.astype(f32).reshape(b, t, h, hd).transpose(1, 0, 2, 3)

    seqs = (to_tm(r), to_tm(decay), to_tm(k), to_tm(v), to_tm(-kk), to_tm(kk * a.astype(f32)))

    def step(S, inp):
        r_t, w_t, k_t, v_t, a_t, b_t = inp
        sa = jnp.einsum('bhij,bhj->bhi', S, a_t)
        S = S * w_t[:, :, None, :] + sa[..., :, None] * b_t[:, :, None, :] + v_t[..., :, None] * k_t[:, :, None, :]
        y_t = jnp.einsum('bhij,bhj->bhi', S, r_t)
        return S, y_t

    S0 = jnp.zeros((b, h, hd, hd), f32)
    _, y = lax.scan(step, S0, seqs)
    y = y.transpose(1, 0, 2, 3)
    mean = jnp.mean(y, axis=-1, keepdims=True)
    var = jnp.mean(jnp.square(y - mean), axis=-1, keepdims=True)
    y = ((y - mean) * lax.rsqrt(var + GN_EPS)).reshape(b, t, dm)
    y = y * ln_w.astype(f32) + ln_b.astype(f32)
    r4 = r.astype(f32).reshape(b, t, h, hd)
    k4 = k.astype(f32).reshape(b, t, h, hd)
    v4 = v.astype(f32).reshape(b, t, h, hd)
    bonus = jnp.sum(r4 * k4 * r_k.astype(f32), axis=-1, keepdims=True) * v4
    y = y + bonus.reshape(b, t, dm)
    return (y * gate.astype(f32)).astype(xn.dtype) @ w_o


def _swiglu(xn, w_in, w_out):
    gu = xn @ w_in
    g, u = gu[..., :FFN_HIDDEN], gu[..., FFN_HIDDEN:]
    return (jax.nn.silu(g) * u) @ w_out


def setup_inputs(seed: int = 0) -> dict:
    key = jax.random.key(seed)
    k = jax.random.split(key, 27)
    f32 = jnp.float32
    D = D_MODEL
    na, npool, nc = _n_layers_of(0), _n_layers_of(1), _n_layers_of(2)
    qkv_out = (ATTN_Q_HEADS + 2 * ATTN_KV_HEADS) * ATTN_HEAD_DIM

    def nrm(kk, shape, s):
        return jax.random.normal(kk, shape, f32) * s

    return {
        'x': nrm(k[0], (BATCH, SEQ, D), 1.0),
        'norm_mix_g': 1.0 + nrm(k[1], (DEPTH, D), 0.02),
        'norm_ffn_g': 1.0 + nrm(k[2], (DEPTH, D), 0.02),
        'final_norm_g': 1.0 + nrm(k[3], (D,), 0.02),
        'ffn_w_in': nrm(k[4], (DEPTH, D, 2 * FFN_HIDDEN), D ** -0.5),
        'ffn_w_out': nrm(k[5], (DEPTH, FFN_HIDDEN, D), FFN_HIDDEN ** -0.5),
        'attn_w_qkv': nrm(k[6], (na, D, qkv_out), D ** -0.5),
        'attn_w_o': nrm(k[7], (na, ATTN_Q_HEADS * ATTN_HEAD_DIM, D), (ATTN_Q_HEADS * ATTN_HEAD_DIM) ** -0.5),
        'attn_sinks': nrm(k[8], (na, ATTN_Q_HEADS), 1.0),
        'pool_w': nrm(k[9], (npool, POOL_GROUPS, POOL_GROUP_DIM, POOL_GROUP_DIM), POOL_GROUP_DIM ** -0.5),
        'pool_scale': 1.0 + nrm(k[10], (npool, D), 0.1),
        'rwkv_mu': jax.random.uniform(k[11], (nc, 6, D), f32, 0.0, 1.0),
        'rwkv_w_rkv': nrm(k[12], (nc, 3, D, D), D ** -0.5),
        'rwkv_w0': jax.random.uniform(k[13], (nc, D), f32, -6.0, 1.0),
        'rwkv_w1': nrm(k[14], (nc, D, DECAY_LORA), D ** -0.5),
        'rwkv_w2': nrm(k[15], (nc, DECAY_LORA, D), 0.5 * DECAY_LORA ** -0.5),
        'rwkv_a0': nrm(k[16], (nc, D), 0.5),
        'rwkv_a1': nrm(k[17], (nc, D, ICLR_LORA), D ** -0.5),
        'rwkv_a2': nrm(k[18], (nc, ICLR_LORA, D), 0.5 * ICLR_LORA ** -0.5),
        'rwkv_g1': nrm(k[19], (nc, D, GATE_LORA), D ** -0.5),
        'rwkv_g2': nrm(k[20], (nc, GATE_LORA, D), GATE_LORA ** -0.5),
        'rwkv_k_k': 0.85 + nrm(k[21], (nc, D), 0.05),
        'rwkv_k_a': 1.0 + nrm(k[22], (nc, D), 0.05),
        'rwkv_r_k': nrm(k[23], (nc, RWKV_HEADS, RWKV_HEAD_DIM), 0.1),
        'rwkv_ln_w': 1.0 + nrm(k[24], (nc, D), 0.02),
        'rwkv_ln_b': nrm(k[25], (nc, D), 0.02),
        'rwkv_w_o': nrm(k[26], (nc, D, D), D ** -0.5),
    }


def reference(x, norm_mix_g, norm_ffn_g, final_norm_g, ffn_w_in, ffn_w_out, attn_w_qkv, attn_w_o, attn_sinks,
              pool_w, pool_scale, rwkv_mu, rwkv_w_rkv, rwkv_w0, rwkv_w1, rwkv_w2, rwkv_a0, rwkv_a1, rwkv_a2,
              rwkv_g1, rwkv_g2, rwkv_k_k, rwkv_k_a, rwkv_r_k, rwkv_ln_w, rwkv_ln_b, rwkv_w_o):
    h = x
    for i in range(DEPTH):
        kind, j = i % N_MIXERS, i // N_MIXERS
        xn = _rmsnorm(h, norm_mix_g[i])
        if kind == 0:
            mix = _swa_sink_attention(xn, attn_w_qkv[j], attn_w_o[j], attn_sinks[j])
        elif kind == 1:
            mix = _multiscale_pool(xn, pool_w[j], pool_scale[j])
        else:
            mix = _rwkv7_time_mix(xn, rwkv_mu[j], rwkv_w_rkv[j], rwkv_w0[j], rwkv_w1[j], rwkv_w2[j],
                                  rwkv_a0[j], rwkv_a1[j], rwkv_a2[j], rwkv_g1[j], rwkv_g2[j],
                                  rwkv_k_k[j], rwkv_k_a[j], rwkv_r_k[j], rwkv_ln_w[j], rwkv_ln_b[j], rwkv_w_o[j])
        h = h + mix
        h = h + _swiglu(_rmsnorm(h, norm_ffn_g[i]), ffn_w_in[i], ffn_w_out[i])
    return _rmsnorm(h, final_norm_g)
```

```python
import functools

import jax
import jax.numpy as jnp
from jax import lax
from jax.experimental import pallas as pl
from jax.experimental.pallas import tpu as pltpu

F32 = jnp.float32
BF16 = jnp.bfloat16

RMS_EPS = 1e-6
GN_EPS = 64e-5
ATTN_HEAD_DIM = 64
ATTN_Q_HEADS = 16
ATTN_KV_HEADS = 4
ATTN_GROUP = ATTN_Q_HEADS // ATTN_KV_HEADS
ATTN_BLOCK = 128
POOL_WINDOWS = (2, 4, 8, 16)
POOL_HALO = 16
RWKV_HEAD_DIM = 64
CHUNK = 64

VMEM_LIMIT_BYTES = 56 * 1024 * 1024

NN = (((1,), (0,)), ((), ()))
NT = (((1,), (1,)), ((), ()))
TN = (((0,), (0,)), ((), ()))


def _dot(a, b, dims=NN):
    return lax.dot_general(a.astype(BF16), b.astype(BF16), dims, preferred_element_type=F32)


def _rms(x, g):
    return x * lax.rsqrt(jnp.mean(x * x, axis=-1, keepdims=True) + RMS_EPS) * g


def _params(sem):
    return pltpu.CompilerParams(dimension_semantics=sem, vmem_limit_bytes=VMEM_LIMIT_BYTES)


def _resident(shape):
    nd = len(shape)
    return pl.BlockSpec(shape, lambda *_: (0,) * nd, pipeline_mode=pl.Buffered(1))


def _ffn_kernel(h_ref, g_ref, win_ref, wout_ref, gf_ref, o_ref, act_ref, *, hidden, tf, final):
    h = h_ref[...]
    xn = _rms(h, g_ref[...]).astype(BF16)
    for j in range(hidden // tf):
        gate = jnp.dot(xn, win_ref[:, j * tf:(j + 1) * tf], preferred_element_type=F32)
        up = jnp.dot(xn, win_ref[:, hidden + j * tf:hidden + (j + 1) * tf], preferred_element_type=F32)
        act_ref[:, j * tf:(j + 1) * tf] = (gate * jax.nn.sigmoid(gate) * up).astype(BF16)
    out = h + jnp.dot(act_ref[...], wout_ref[...], preferred_element_type=F32)
    if final:
        out = _rms(out, gf_ref[...])
    o_ref[...] = out


def _ffn(h, g, w_in, w_out, g_final, *, final, tm=512, tf=256):
    n, d = h.shape
    hidden = w_out.shape[0]
    row = pl.BlockSpec((tm, d), lambda i: (i, 0))
    return pl.pallas_call(
        functools.partial(_ffn_kernel, hidden=hidden, tf=tf, final=final),
        grid=(n // tm,),
        in_specs=[row, _resident((1, d)), _resident(w_in.shape), _resident(w_out.shape), _resident((1, d))],
        out_specs=row,
        out_shape=jax.ShapeDtypeStruct((n, d), F32),
        scratch_shapes=[pltpu.VMEM((tm, hidden), BF16)],
        compiler_params=_params(("parallel",)),
        name="ffn",
    )(h, g.reshape(1, d), w_in, w_out, g_final.reshape(1, d))


def _qkv_kernel(h_ref, g_ref, w_ref, q_ref, k_ref, v_ref, *, qd, kvd, scale):
    xn = _rms(h_ref[...], g_ref[...]).astype(BF16)
    qkv = jnp.dot(xn, w_ref[...], preferred_element_type=F32)
    q_ref[...] = (qkv[:, :qd] * scale).astype(BF16)
    k_ref[...] = qkv[:, qd:qd + kvd].astype(BF16)
    v_ref[...] = qkv[:, qd + kvd:].astype(BF16)


def _qkv(h, g, w_qkv, *, tm=512):
    n, d = h.shape
    qd = ATTN_Q_HEADS * ATTN_HEAD_DIM
    kvd = ATTN_KV_HEADS * ATTN_HEAD_DIM
    return pl.pallas_call(
        functools.partial(_qkv_kernel, qd=qd, kvd=kvd, scale=ATTN_HEAD_DIM ** -0.5),
        grid=(n // tm,),
        in_specs=[pl.BlockSpec((tm, d), lambda i: (i, 0)), _resident((1, d)), _resident(w_qkv.shape)],
        out_specs=[pl.BlockSpec((tm, qd), lambda i: (i, 0)),
                   pl.BlockSpec((tm, kvd), lambda i: (i, 0)),
                   pl.BlockSpec((tm, kvd), lambda i: (i, 0))],
        out_shape=[jax.ShapeDtypeStruct((n, qd), BF16),
                   jax.ShapeDtypeStruct((n, kvd), BF16),
                   jax.ShapeDtypeStruct((n, kvd), BF16)],
        compiler_params=_params(("parallel",)),
        name="attn_qkv",
    )(h, g.reshape(1, d), w_qkv)


def _attn_kernel(q_ref, kp_ref, kc_ref, vp_ref, vc_ref, sink_ref, h_ref, wo_ref, o_ref, oh_ref, *, blocks_per_seq):
    blk = ATTN_BLOCK
    hd = ATTN_HEAD_DIM
    first = (pl.program_id(0) % blocks_per_seq) == 0
    qpos = lax.broadcasted_iota(jnp.int32, (blk, 2 * blk), 0) + blk
    kpos = lax.broadcasted_iota(jnp.int32, (blk, 2 * blk), 1)
    diff = qpos - kpos
    valid = (diff >= 0) & (diff < blk) & (jnp.logical_not(first) | (kpos >= blk))
    for hkv in range(ATTN_KV_HEADS):
        ksl = slice(hkv * hd, (hkv + 1) * hd)
        kb = jnp.concatenate([kp_ref[:, ksl], kc_ref[:, ksl]], axis=0)
        vb = jnp.concatenate([vp_ref[:, ksl], vc_ref[:, ksl]], axis=0)
        for g in range(ATTN_GROUP):
            hq = hkv * ATTN_GROUP + g
            qsl = slice(hq * hd, (hq + 1) * hd)
            s = lax.dot_general(q_ref[:, qsl], kb, NT, preferred_element_type=F32)
            s = jnp.where(valid, s, -jnp.inf)
            sink = sink_ref[:, hq:hq + 1]
            m = jnp.maximum(jnp.max(s, axis=-1, keepdims=True), sink)
            p = jnp.exp(s - m)
            den = jnp.sum(p, axis=-1, keepdims=True) + jnp.exp(sink - m)
            o = jnp.dot(p.astype(BF16), vb, preferred_element_type=F32)
            oh_ref[:, qsl] = (o / den).astype(BF16)
    o_ref[...] = h_ref[...] + jnp.dot(oh_ref[...], wo_ref[...], preferred_element_type=F32)


def _attention(h, q, k, v, sinks, w_o, *, seq):
    n, d = h.shape
    blk = ATTN_BLOCK
    qd = q.shape[1]
    kvd = k.shape[1]
    bps = seq // blk
    cur = lambda i: (i, 0)
    prev = lambda i: (jnp.maximum(i - 1, 0), 0)
    return pl.pallas_call(
        functools.partial(_attn_kernel, blocks_per_seq=bps),
        grid=(n // blk,),
        in_specs=[pl.BlockSpec((blk, qd), cur),
                  pl.BlockSpec((blk, kvd), prev), pl.BlockSpec((blk, kvd), cur),
                  pl.BlockSpec((blk, kvd), prev), pl.BlockSpec((blk, kvd), cur),
                  _resident((1, ATTN_Q_HEADS)),
                  pl.BlockSpec((blk, d), cur),
                  _resident(w_o.shape)],
        out_specs=pl.BlockSpec((blk, d), cur),
        out_shape=jax.ShapeDtypeStruct((n, d), F32),
        scratch_shapes=[pltpu.VMEM((blk, qd), BF16)],
        compiler_params=_params(("parallel",)),
        name="attn_core",
    )(q, k, k, v, v, sinks.reshape(1, -1), h, w_o)


def _pool_kernel(hp_ref, hc_ref, g_ref, pw_ref, ps_ref, o_ref, *, tiles_per_seq, tm):
    halo = POOL_HALO
    gdim = pw_ref.shape[1]
    tile = pl.program_id(0) % tiles_per_seq
    h = hc_ref[...]
    g = g_ref[...]
    keep = (tile > 0).astype(F32)
    ext = jnp.concatenate([_rms(hp_ref[...], g) * keep, _rms(h, g)], axis=0)
    pos = tile * tm + lax.broadcasted_iota(jnp.int32, (tm, 1), 0)
    for gi, w in enumerate(POOL_WINDOWS):
        csl = slice(gi * gdim, (gi + 1) * gdim)
        x = ext[:, csl]
        s = x
        span = 1
        while span < w:
            s = s + pltpu.roll(s, span, axis=0)
            span *= 2
        cnt = jnp.minimum(pos + 1, w).astype(F32)
        dpool = s[halo:, :] / cnt - x[halo:, :]
        y = jnp.dot(dpool.astype(BF16), pw_ref[gi], preferred_element_type=F32)
        o_ref[:, csl] = h[:, csl] + y * ps_ref[:, csl]


def _pool(h, g, pool_w, pool_scale, *, seq, tm=512):
    n, d = h.shape
    halo = POOL_HALO
    per = tm // halo
    return pl.pallas_call(
        functools.partial(_pool_kernel, tiles_per_seq=seq // tm, tm=tm),
        grid=(n // tm,),
        in_specs=[pl.BlockSpec((halo, d), lambda i: (jnp.maximum(i * per - 1, 0), 0)),
                  pl.BlockSpec((tm, d), lambda i: (i, 0)),
                  _resident((1, d)), _resident(pool_w.shape), _resident((1, d))],
        out_specs=pl.BlockSpec((tm, d), lambda i: (i, 0)),
        out_shape=jax.ShapeDtypeStruct((n, d), F32),
        compiler_params=_params(("parallel",)),
        name="pool_mix",
    )(h, h, g.reshape(1, d), pool_w, pool_scale.reshape(1, d))


def _rwkv_proj_kernel(hp_ref, hc_ref, g_ref, mu_ref, wrkv_ref, w0_ref, w1_ref, w2_ref, a0_ref, a1_ref, a2_ref,
                      g1_ref, g2_ref, kk_ref, ka_ref,
                      r_ref, ld_ref, k_ref, v_ref, kku_ref, a_ref, gate_ref, *, tiles_per_seq, tm):
    tile = pl.program_id(0) % tiles_per_seq
    g = g_ref[...]
    xn = _rms(hc_ref[...], g)
    last = _rms(hp_ref[...], g)[-1:, :] * (tile > 0).astype(F32)
    rowid = lax.broadcasted_iota(jnp.int32, (tm, 1), 0)
    shifted = jnp.where(rowid == 0, last, pltpu.roll(xn, 1, axis=0))
    dx = shifted - xn

    def mix(c):
        return (xn + dx * mu_ref[c:c + 1, :]).astype(BF16)

    r = jnp.dot(mix(0), wrkv_ref[0], preferred_element_type=F32)
    k = jnp.dot(mix(1), wrkv_ref[1], preferred_element_type=F32)
    v = jnp.dot(mix(2), wrkv_ref[2], preferred_element_type=F32)
    wl = w0_ref[...] + _dot(jnp.tanh(jnp.dot(mix(3), w1_ref[...], preferred_element_type=F32)), w2_ref[...])
    w_log = -jax.nn.softplus(-wl) - 0.5
    a = jax.nn.sigmoid(a0_ref[...] + _dot(jnp.dot(mix(4), a1_ref[...], preferred_element_type=F32), a2_ref[...]))
    gate = _dot(jax.nn.sigmoid(jnp.dot(mix(5), g1_ref[...], preferred_element_type=F32)), g2_ref[...])
    r_ref[...] = r
    ld_ref[...] = -jnp.exp(w_log)
    k_ref[...] = k * (1.0 + (a - 1.0) * ka_ref[...])
    v_ref[...] = v
    kku_ref[...] = k * kk_ref[...]
    a_ref[...] = a
    gate_ref[...] = gate


def _rwkv_proj(h, g, mu, w_rkv, w0, w1, w2, a0, a1, a2, g1, g2, k_k, k_a, *, seq, tm=512):
    n, d = h.shape
    row = pl.BlockSpec((tm, d), lambda i: (i, 0))
    per = tm // 8
    vec = lambda x: x.reshape(1, d)
    outs = [jax.ShapeDtypeStruct((n, d), F32)] * 7
    return pl.pallas_call(
        functools.partial(_rwkv_proj_kernel, tiles_per_seq=seq // tm, tm=tm),
        grid=(n // tm,),
        in_specs=[pl.BlockSpec((8, d), lambda i: (jnp.maximum(i * per - 1, 0), 0)), row,
                  _resident((1, d)), _resident(mu.shape), _resident(w_rkv.shape),
                  _resident((1, d)), _resident(w1.shape), _resident(w2.shape),
                  _resident((1, d)), _resident(a1.shape), _resident(a2.shape),
                  _resident(g1.shape), _resident(g2.shape), _resident((1, d)), _resident((1, d))],
        out_specs=[row] * 7,
        out_shape=outs,
        compiler_params=_params(("parallel",)),
        name="rwkv_proj",
    )(h, h, vec(g), mu, w_rkv, vec(w0), w1, w2, vec(a0), a1, a2, g1, g2, vec(k_k), vec(k_a))


def _head_chunk(s0, r, k, v, at, bt, kt, rt, bh, kh, decay_total, tri_incl, tri_strict, eye):
    a_ab = tri_strict * _dot(at, bt, NT)
    a_ak = tri_strict * _dot(at, kt, NT)
    m_rb = tri_incl * _dot(rt, bt, NT)
    m_rk = tri_incl * _dot(rt, kt, NT)
    inv = eye + a_ab
    power = a_ab
    for _ in range(5):
        power = _dot(power, power)
        inv = inv + _dot(inv, power)
    a_hat = _dot(inv, at)
    u0 = _dot(inv, _dot(a_ak, v))
    r_hat = rt + _dot(m_rb, a_hat)
    y0 = _dot(m_rb, u0) + _dot(m_rk, v)
    u = _dot(a_hat, s0, NT) + u0
    y = _dot(r_hat, s0, NT) + y0
    s1 = s0 * decay_total + _dot(u, bh, TN) + _dot(v, kh, TN)
    return s1, y


def _cumsum_rows(tri_incl_bf16, x):
    hi = x.astype(BF16)
    rem = x - hi.astype(F32)
    mid = rem.astype(BF16)
    lo = (rem - mid.astype(F32)).astype(BF16)
    f = lambda z: jnp.dot(tri_incl_bf16, z, preferred_element_type=F32)
    return f(hi) + f(mid) + f(lo)


def _rwkv_scan_kernel(r_ref, ld_ref, k_ref, v_ref, kku_ref, a_ref, rk_ref, lnw_ref, lnb_ref, y_ref, s_ref,
                      *, chunks):
    hd = RWKV_HEAD_DIM
    c = CHUNK
    heads = r_ref.shape[1] // hd

    @pl.when(pl.program_id(2) == 0)
    def _():
        s_ref[...] = jnp.zeros_like(s_ref)

    row = lax.broadcasted_iota(jnp.int32, (c, c), 0)
    col = lax.broadcasted_iota(jnp.int32, (c, c), 1)
    tri_incl = (row >= col).astype(F32)
    tri_strict = (row > col).astype(F32)
    eye = (row == col).astype(F32)
    tri_incl_bf16 = tri_incl.astype(BF16)
    lane_head = lax.broadcasted_iota(jnp.int32, (c, heads * hd), 1) // hd

    def per_head(fn):
        out = None
        for hh in range(heads):
            val = fn(slice(hh * hd, (hh + 1) * hd))
            out = val if out is None else jnp.where(lane_head == hh, val, out)
        return out

    def chunk_body(ci, carry):
        rows = pl.ds(pl.multiple_of(ci * c, c), c)
        r = r_ref[rows, :]
        ld = ld_ref[rows, :]
        k = k_ref[rows, :]
        v = v_ref[rows, :]
        kku = kku_ref[rows, :]
        lr = a_ref[rows, :]
        sq = kku * kku
        norm = per_head(lambda sl: jnp.sqrt(jnp.sum(sq[:, sl], axis=-1, keepdims=True)))
        kk = kku / jnp.maximum(norm, 1e-12)
        cum = _cumsum_rows(tri_incl_bf16, ld)
        total = cum[c - 1:c, :]
        b = kk * lr
        e_neg = jnp.exp(-cum)
        e_tot = jnp.exp(total - cum)
        at = -kk * jnp.exp(cum - ld)
        bt = b * e_neg
        kt = k * e_neg
        rt = r * jnp.exp(cum)
        bh = b * e_tot
        kh = k * e_tot
        decay_total = jnp.exp(total)
        ys = []
        for hh in range(heads):
            sl = slice(hh * hd, (hh + 1) * hd)
            s1, y = _head_chunk(s_ref[hh], r[:, sl], k[:, sl], v[:, sl], at[:, sl], bt[:, sl], kt[:, sl],
                                rt[:, sl], bh[:, sl], kh[:, sl], decay_total[:, sl], tri_incl, tri_strict, eye)
            s_ref[hh] = s1
            ys.append(y)
        y = jnp.concatenate(ys, axis=1)
        mean = per_head(lambda sl: jnp.mean(y[:, sl], axis=-1, keepdims=True))
        yc = y - mean
        ycsq = yc * yc
        var = per_head(lambda sl: jnp.mean(ycsq[:, sl], axis=-1, keepdims=True))
        yn = yc * lax.rsqrt(var + GN_EPS) * lnw_ref[...] + lnb_ref[...]
        rkr = r * k * rk_ref[...]
        bonus = per_head(lambda sl: jnp.sum(rkr[:, sl], axis=-1, keepdims=True)) * v
        y_ref[rows, :] = yn + bonus
        return carry

    lax.fori_loop(0, chunks, chunk_body, 0)


def _rwkv_scan(r, ld, k, v, kku, a, r_k, ln_w, ln_b, *, batch, seq, tb=256, lanes=128):
    n, d = r.shape
    hd = RWKV_HEAD_DIM
    heads = lanes // hd
    tpb = seq // tb
    blk = pl.BlockSpec((tb, lanes), lambda b, p, t: (b * tpb + t, p))
    vec = pl.BlockSpec((1, lanes), lambda b, p, t: (0, p))
    return pl.pallas_call(
        functools.partial(_rwkv_scan_kernel, chunks=tb // CHUNK),
        grid=(batch, d // lanes, tpb),
        in_specs=[blk] * 6 + [vec] * 3,
        out_specs=blk,
        out_shape=jax.ShapeDtypeStruct((n, d), F32),
        scratch_shapes=[pltpu.VMEM((heads, hd, hd), F32)],
        compiler_params=_params(("parallel", "parallel", "arbitrary")),
        name="rwkv_scan",
    )(r, ld, k, v, kku, a, r_k.reshape(1, d), ln_w.reshape(1, d), ln_b.reshape(1, d))


def _gated_out_kernel(y_ref, gate_ref, h_ref, wo_ref, o_ref):
    o_ref[...] = h_ref[...] + jnp.dot((y_ref[...] * gate_ref[...]).astype(BF16), wo_ref[...],
                                      preferred_element_type=F32)


def _gated_out(y, gate, h, w_o, *, tm=512):
    n, d = h.shape
    row = pl.BlockSpec((tm, d), lambda i: (i, 0))
    return pl.pallas_call(
        _gated_out_kernel,
        grid=(n // tm,),
        in_specs=[row, row, row, _resident(w_o.shape)],
        out_specs=row,
        out_shape=jax.ShapeDtypeStruct((n, d), F32),
        compiler_params=_params(("parallel",)),
        name="rwkv_out",
    )(y, gate, h, w_o)


def kernel(x, norm_mix_g, norm_ffn_g, final_norm_g, ffn_w_in, ffn_w_out, attn_w_qkv, attn_w_o, attn_sinks, pool_w, pool_scale, rwkv_mu, rwkv_w_rkv, rwkv_w0, rwkv_w1, rwkv_w2, rwkv_a0, rwkv_a1, rwkv_a2, rwkv_g1, rwkv_g2, rwkv_k_k, rwkv_k_a, rwkv_r_k, rwkv_ln_w, rwkv_ln_b, rwkv_w_o):
    batch, seq, d = x.shape
    depth = norm_mix_g.shape[0]
    n_mixers = 3
    bf = lambda w: w.astype(BF16)
    h = x.reshape(batch * seq, d)
    for i in range(depth):
        kind, j = i % n_mixers, i // n_mixers
        g = norm_mix_g[i]
        if kind == 0:
            q, k, v = _qkv(h, g, bf(attn_w_qkv[j]))
            h = _attention(h, q, k, v, attn_sinks[j], bf(attn_w_o[j]), seq=seq)
        elif kind == 1:
            h = _pool(h, g, bf(pool_w[j]), pool_scale[j], seq=seq)
        else:
            r, ld, k, v, kku, a, gate = _rwkv_proj(
                h, g, rwkv_mu[j], bf(rwkv_w_rkv[j]), rwkv_w0[j], bf(rwkv_w1[j]), bf(rwkv_w2[j]), rwkv_a0[j],
                bf(rwkv_a1[j]), bf(rwkv_a2[j]), bf(rwkv_g1[j]), bf(rwkv_g2[j]), rwkv_k_k[j], rwkv_k_a[j], seq=seq)
            y = _rwkv_scan(r, ld, k, v, kku, a, rwkv_r_k[j], rwkv_ln_w[j], rwkv_ln_b[j], batch=batch, seq=seq)
            h = _gated_out(y, gate, h, bf(rwkv_w_o[j]))
        h = _ffn(h, norm_ffn_g[i], bf(ffn_w_in[i]), bf(ffn_w_out[i]), final_norm_g, final=(i == depth - 1))
    return h.reshape(batch, seq, d)
```

```python
import functools

import jax
import jax.numpy as jnp
from jax import lax
from jax.experimental import pallas as pl
from jax.experimental.pallas import tpu as pltpu

F32 = jnp.float32
BF16 = jnp.bfloat16

RMS_EPS = 1e-6
GN_EPS = 64e-5
ATTN_HEAD_DIM = 64
ATTN_Q_HEADS = 16
ATTN_KV_HEADS = 4
ATTN_GROUP = ATTN_Q_HEADS // ATTN_KV_HEADS
ATTN_BLOCK = 128
POOL_WINDOWS = (2, 4, 8, 16)
POOL_HALO = 16
RWKV_HEAD_DIM = 64
CHUNK = 64

VMEM_LIMIT_BYTES = 56 * 1024 * 1024

NN = (((1,), (0,)), ((), ()))
NT = (((1,), (1,)), ((), ()))
TN = (((0,), (0,)), ((), ()))


def _dot(a, b, dims=NN):
    return lax.dot_general(a.astype(BF16), b.astype(BF16), dims, preferred_element_type=F32)


def _rms(x, g):
    return x * lax.rsqrt(jnp.mean(x * x, axis=-1, keepdims=True) + RMS_EPS) * g


def _params(sem):
    return pltpu.CompilerParams(dimension_semantics=sem, vmem_limit_bytes=VMEM_LIMIT_BYTES)


def _resident(shape):
    nd = len(shape)
    return pl.BlockSpec(shape, lambda *_: (0,) * nd, pipeline_mode=pl.Buffered(1))


def _ffn_kernel(h_ref, g_ref, win_ref, wout_ref, gf_ref, o_ref, act_ref, *, hidden, tf, final):
    h = h_ref[...]
    xn = _rms(h, g_ref[...]).astype(BF16)
    for j in range(hidden // tf):
        gate = jnp.dot(xn, win_ref[:, j * tf:(j + 1) * tf], preferred_element_type=F32)
        up = jnp.dot(xn, win_ref[:, hidden + j * tf:hidden + (j + 1) * tf], preferred_element_type=F32)
        act_ref[:, j * tf:(j + 1) * tf] = (gate * jax.nn.sigmoid(gate) * up).astype(BF16)
    out = h + jnp.dot(act_ref[...], wout_ref[...], preferred_element_type=F32)
    if final:
        out = _rms(out, gf_ref[...])
    o_ref[...] = out


def _ffn(h, g, w_in, w_out, g_final, *, final, tm=512, tf=256):
    n, d = h.shape
    hidden = w_out.shape[0]
    row = pl.BlockSpec((tm, d), lambda i: (i, 0))
    return pl.pallas_call(
        functools.partial(_ffn_kernel, hidden=hidden, tf=tf, final=final),
        grid=(n // tm,),
        in_specs=[row, _resident((1, d)), _resident(w_in.shape), _resident(w_out.shape), _resident((1, d))],
        out_specs=row,
        out_shape=jax.ShapeDtypeStruct((n, d), F32),
        scratch_shapes=[pltpu.VMEM((tm, hidden), BF16)],
        compiler_params=_params(("parallel",)),
        name="ffn",
    )(h, g.reshape(1, d), w_in, w_out, g_final.reshape(1, d))


def _qkv_kernel(h_ref, g_ref, w_ref, q_ref, k_ref, v_ref, *, qd, kvd, scale):
    xn = _rms(h_ref[...], g_ref[...]).astype(BF16)
    qkv = jnp.dot(xn, w_ref[...], preferred_element_type=F32)
    q_ref[...] = (qkv[:, :qd] * scale).astype(BF16)
    k_ref[...] = qkv[:, qd:qd + kvd].astype(BF16)
    v_ref[...] = qkv[:, qd + kvd:].astype(BF16)


def _qkv(h, g, w_qkv, *, tm=512):
    n, d = h.shape
    qd = ATTN_Q_HEADS * ATTN_HEAD_DIM
    kvd = ATTN_KV_HEADS * ATTN_HEAD_DIM
    return pl.pallas_call(
        functools.partial(_qkv_kernel, qd=qd, kvd=kvd, scale=ATTN_HEAD_DIM ** -0.5),
        grid=(n // tm,),
        in_specs=[pl.BlockSpec((tm, d), lambda i: (i, 0)), _resident((1, d)), _resident(w_qkv.shape)],
        out_specs=[pl.BlockSpec((tm, qd), lambda i: (i, 0)),
                   pl.BlockSpec((tm, kvd), lambda i: (i, 0)),
                   pl.BlockSpec((tm, kvd), lambda i: (i, 0))],
        out_shape=[jax.ShapeDtypeStruct((n, qd), BF16),
                   jax.ShapeDtypeStruct((n, kvd), BF16),
                   jax.ShapeDtypeStruct((n, kvd), BF16)],
        compiler_params=_params(("parallel",)),
        name="attn_qkv",
    )(h, g.reshape(1, d), w_qkv)


def _attn_kernel(q_ref, kp_ref, kc_ref, vp_ref, vc_ref, sink_ref, h_ref, wo_ref, o_ref, oh_ref, *, blocks_per_seq):
    blk = ATTN_BLOCK
    hd = ATTN_HEAD_DIM
    first = (pl.program_id(0) % blocks_per_seq) == 0
    qpos = lax.broadcasted_iota(jnp.int32, (blk, 2 * blk), 0) + blk
    kpos = lax.broadcasted_iota(jnp.int32, (blk, 2 * blk), 1)
    diff = qpos - kpos
    valid = (diff >= 0) & (diff < blk) & (jnp.logical_not(first) | (kpos >= blk))
    for hkv in range(ATTN_KV_HEADS):
        ksl = slice(hkv * hd, (hkv + 1) * hd)
        kb = jnp.concatenate([kp_ref[:, ksl], kc_ref[:, ksl]], axis=0)
        vb = jnp.concatenate([vp_ref[:, ksl], vc_ref[:, ksl]], axis=0)
        for g in range(ATTN_GROUP):
            hq = hkv * ATTN_GROUP + g
            qsl = slice(hq * hd, (hq + 1) * hd)
            s = lax.dot_general(q_ref[:, qsl], kb, NT, preferred_element_type=F32)
            s = jnp.where(valid, s, -jnp.inf)
            sink = sink_ref[:, hq:hq + 1]
            m = jnp.maximum(jnp.max(s, axis=-1, keepdims=True), sink)
            p = jnp.exp(s - m)
            den = jnp.sum(p, axis=-1, keepdims=True) + jnp.exp(sink - m)
            o = jnp.dot(p.astype(BF16), vb, preferred_element_type=F32)
            oh_ref[:, qsl] = (o / den).astype(BF16)
    o_ref[...] = h_ref[...] + jnp.dot(oh_ref[...], wo_ref[...], preferred_element_type=F32)


def _attention(h, q, k, v, sinks, w_o, *, seq):
    n, d = h.shape
    blk = ATTN_BLOCK
    qd = q.shape[1]
    kvd = k.shape[1]
    bps = seq // blk
    cur = lambda i: (i, 0)
    prev = lambda i: (jnp.maximum(i - 1, 0), 0)
    return pl.pallas_call(
        functools.partial(_attn_kernel, blocks_per_seq=bps),
        grid=(n // blk,),
        in_specs=[pl.BlockSpec((blk, qd), cur),
                  pl.BlockSpec((blk, kvd), prev), pl.BlockSpec((blk, kvd), cur),
                  pl.BlockSpec((blk, kvd), prev), pl.BlockSpec((blk, kvd), cur),
                  _resident((1, ATTN_Q_HEADS)),
                  pl.BlockSpec((blk, d), cur),
                  _resident(w_o.shape)],
        out_specs=pl.BlockSpec((blk, d), cur),
        out_shape=jax.ShapeDtypeStruct((n, d), F32),
        scratch_shapes=[pltpu.VMEM((blk, qd), BF16)],
        compiler_params=_params(("parallel",)),
        name="attn_core",
    )(q, k, k, v, v, sinks.reshape(1, -1), h, w_o)


def _pool_kernel(hp_ref, hc_ref, g_ref, pw_ref, ps_ref, o_ref, *, tiles_per_seq, tm):
    halo = POOL_HALO
    gdim = pw_ref.shape[1]
    tile = pl.program_id(0) % tiles_per_seq
    h = hc_ref[...]
    g = g_ref[...]
    keep = (tile > 0).astype(F32)
    ext = jnp.concatenate([_rms(hp_ref[...], g) * keep, _rms(h, g)], axis=0)
    pos = tile * tm + lax.broadcasted_iota(jnp.int32, (tm, 1), 0)
    for gi, w in enumerate(POOL_WINDOWS):
        csl = slice(gi * gdim, (gi + 1) * gdim)
        x = ext[:, csl]
        s = x
        span = 1
        while span < w:
            s = s + pltpu.roll(s, span, axis=0)
            span *= 2
        cnt = jnp.minimum(pos + 1, w).astype(F32)
        dpool = s[halo:, :] / cnt - x[halo:, :]
        y = jnp.dot(dpool.astype(BF16), pw_ref[gi], preferred_element_type=F32)
        o_ref[:, csl] = h[:, csl] + y * ps_ref[:, csl]


def _pool(h, g, pool_w, pool_scale, *, seq, tm=512):
    n, d = h.shape
    halo = POOL_HALO
    per = tm // halo
    return pl.pallas_call(
        functools.partial(_pool_kernel, tiles_per_seq=seq // tm, tm=tm),
        grid=(n // tm,),
        in_specs=[pl.BlockSpec((halo, d), lambda i: (jnp.maximum(i * per - 1, 0), 0)),
                  pl.BlockSpec((tm, d), lambda i: (i, 0)),
                  _resident((1, d)), _resident(pool_w.shape), _resident((1, d))],
        out_specs=pl.BlockSpec((tm, d), lambda i: (i, 0)),
        out_shape=jax.ShapeDtypeStruct((n, d), F32),
        compiler_params=_params(("parallel",)),
        name="pool_mix",
    )(h, h, g.reshape(1, d), pool_w, pool_scale.reshape(1, d))


def _rwkv_proj_kernel(hp_ref, hc_ref, g_ref, mu_ref, wrkv_ref, w0_ref, w1_ref, w2_ref, a0_ref, a1_ref, a2_ref,
                      g1_ref, g2_ref, kk_ref, ka_ref,
                      r_ref, ld_ref, k_ref, v_ref, kku_ref, a_ref, gate_ref, *, tiles_per_seq, tm):
    tile = pl.program_id(0) % tiles_per_seq
    g = g_ref[...]
    xn = _rms(hc_ref[...], g)
    last = _rms(hp_ref[...], g)[-1:, :] * (tile > 0).astype(F32)
    rowid = lax.broadcasted_iota(jnp.int32, (tm, 1), 0)
    shifted = jnp.where(rowid == 0, last, pltpu.roll(xn, 1, axis=0))
    dx = shifted - xn

    def mix(c):
        return (xn + dx * mu_ref[c:c + 1, :]).astype(BF16)

    r = jnp.dot(mix(0), wrkv_ref[0], preferred_element_type=F32)
    k = jnp.dot(mix(1), wrkv_ref[1], preferred_element_type=F32)
    v = jnp.dot(mix(2), wrkv_ref[2], preferred_element_type=F32)
    wl = w0_ref[...] + _dot(jnp.tanh(jnp.dot(mix(3), w1_ref[...], preferred_element_type=F32)), w2_ref[...])
    w_log = -jax.nn.softplus(-wl) - 0.5
    a = jax.nn.sigmoid(a0_ref[...] + _dot(jnp.dot(mix(4), a1_ref[...], preferred_element_type=F32), a2_ref[...]))
    gate = _dot(jax.nn.sigmoid(jnp.dot(mix(5), g1_ref[...], preferred_element_type=F32)), g2_ref[...])
    r_ref[...] = r
    ld_ref[...] = -jnp.exp(w_log)
    k_ref[...] = k * (1.0 + (a - 1.0) * ka_ref[...])
    v_ref[...] = v
    kku_ref[...] = k * kk_ref[...]
    a_ref[...] = a
    gate_ref[...] = gate


def _rwkv_proj(h, g, mu, w_rkv, w0, w1, w2, a0, a1, a2, g1, g2, k_k, k_a, *, seq, tm=512):
    n, d = h.shape
    row = pl.BlockSpec((tm, d), lambda i: (i, 0))
    per = tm // 8
    vec = lambda x: x.reshape(1, d)
    outs = [jax.ShapeDtypeStruct((n, d), F32)] * 7
    return pl.pallas_call(
        functools.partial(_rwkv_proj_kernel, tiles_per_seq=seq // tm, tm=tm),
        grid=(n // tm,),
        in_specs=[pl.BlockSpec((8, d), lambda i: (jnp.maximum(i * per - 1, 0), 0)), row,
                  _resident((1, d)), _resident(mu.shape), _resident(w_rkv.shape),
                  _resident((1, d)), _resident(w1.shape), _resident(w2.shape),
                  _resident((1, d)), _resident(a1.shape), _resident(a2.shape),
                  _resident(g1.shape), _resident(g2.shape), _resident((1, d)), _resident((1, d))],
        out_specs=[row] * 7,
        out_shape=outs,
        compiler_params=_params(("parallel",)),
        name="rwkv_proj",
    )(h, h, vec(g), mu, w_rkv, vec(w0), w1, w2, vec(a0), a1, a2, g1, g2, vec(k_k), vec(k_a))


HEAD_GROUP = 4
GROUP_W = HEAD_GROUP * RWKV_HEAD_DIM
LANE_TILE = 128
assert CHUNK == RWKV_HEAD_DIM and GROUP_W == 2 * LANE_TILE


def _block_rows(x, lo):
    x0, x1 = x[:, :LANE_TILE], x[:, LANE_TILE:]
    zf = jnp.zeros_like(x0)
    z = jnp.zeros(x0.shape, BF16)
    keep = lambda m, t: jnp.where(m, t, zf).astype(BF16)
    hi = jnp.logical_not(lo)
    return jnp.concatenate([
        jnp.concatenate([keep(lo, x0), z], axis=1),
        jnp.concatenate([keep(hi, x0), z], axis=1),
        jnp.concatenate([z, keep(lo, x1)], axis=1),
        jnp.concatenate([z, keep(hi, x1)], axis=1)], axis=0)


def _cumsum_rows(tri_incl_bf16, x):
    hi = x.astype(BF16)
    rem = x - hi.astype(F32)
    mid = rem.astype(BF16)
    lo = (rem - mid.astype(F32)).astype(BF16)
    f = lambda z: jnp.dot(tri_incl_bf16, z, preferred_element_type=F32)
    return f(hi) + f(mid) + f(lo)


def _rwkv_scan_kernel(r_ref, ld_ref, k_ref, v_ref, kku_ref, a_ref, rk_ref, lnw_ref, lnb_ref, y_ref, st_ref,
                      *, chunks):
    hd = RWKV_HEAD_DIM
    c = CHUNK
    gw = GROUP_W
    d = r_ref.shape[1]

    @pl.when(pl.program_id(1) == 0)
    def _():
        st_ref[...] = jnp.zeros_like(st_ref)

    t_idx = lax.broadcasted_iota(jnp.int32, (c, gw), 0)
    s_idx = lax.broadcasted_iota(jnp.int32, (c, gw), 1) % c
    tri_strict = t_idx > s_idx
    tri_incl = t_idx >= s_idx
    eye = (t_idx == s_idx).astype(F32)
    lo = lax.broadcasted_iota(jnp.int32, (c, LANE_TILE), 1) < hd
    same_head = (lax.broadcasted_iota(jnp.int32, (gw, gw), 0) // hd
                 == lax.broadcasted_iota(jnp.int32, (gw, gw), 1) // hd)
    tri_c = (lax.broadcasted_iota(jnp.int32, (c, c), 0) >= lax.broadcasted_iota(jnp.int32, (c, c), 1)).astype(BF16)

    def head_sums(x):
        tiles = []
        for t in range(d // LANE_TILE):
            xt = x[:, t * LANE_TILE:(t + 1) * LANE_TILE]
            s0 = jnp.sum(jnp.where(lo, xt, 0.0), axis=-1, keepdims=True)
            s1 = jnp.sum(jnp.where(lo, 0.0, xt), axis=-1, keepdims=True)
            tiles.append(jnp.where(lo, s0, s1))
        return jnp.concatenate(tiles, axis=1)

    def chunk_body(ci, carry):
        rows = pl.ds(pl.multiple_of(ci * c, c), c)
        r = r_ref[rows, :]
        ld = ld_ref[rows, :]
        k = k_ref[rows, :]
        v = v_ref[rows, :]
        kku = kku_ref[rows, :]
        lr = a_ref[rows, :]
        kk = kku / jnp.maximum(jnp.sqrt(head_sums(kku * kku)), 1e-12)
        cum = _cumsum_rows(tri_c, ld)
        total = cum[c - 1:c, :]
        b = kk * lr
        e_neg = jnp.exp(-cum)
        e_tot = jnp.exp(total - cum)
        at = -kk * jnp.exp(cum - ld)
        bt = b * e_neg
        kt = k * e_neg
        rt = r * jnp.exp(cum)
        bh = b * e_tot
        kh = k * e_tot
        dec = jnp.exp(total)
        groups = range(d // gw)
        sls = [slice(g * gw, (g + 1) * gw) for g in groups]
        stack = lambda top, bottom: jnp.concatenate([top, bottom], axis=0)
        blocks = lambda xs: [_block_rows(x, lo) for x in xs]
        at_g = [at[:, s] for s in sls]
        rt_g = [rt[:, s] for s in sls]
        v_g = [v[:, s] for s in sls]
        ar = [stack(at_g[g], rt_g[g]).astype(BF16) for g in groups]
        bt_b = blocks([bt[:, s] for s in sls])
        kt_b = blocks([kt[:, s] for s in sls])
        ab = [_dot(ar[g], bt_b[g], NT) for g in groups]
        ak = [_dot(ar[g], kt_b[g], NT) for g in groups]
        a_ab = [jnp.where(tri_strict, ab[g][:c], 0.0) for g in groups]
        m_rb = [jnp.where(tri_incl, ab[g][c:], 0.0) for g in groups]
        akmk = [stack(jnp.where(tri_strict, ak[g][:c], 0.0), jnp.where(tri_incl, ak[g][c:], 0.0)) for g in groups]
        inv = [eye + a_ab[g] for g in groups]
        pw_b = blocks(a_ab)
        power = [_dot(a_ab[g], pw_b[g]) for g in groups]
        for _ in range(4):
            pw_b = blocks(power)
            z = [_dot(stack(inv[g], power[g]), pw_b[g]) for g in groups]
            inv = [inv[g] + z[g][:c] for g in groups]
            power = [z[g][c:] for g in groups]
        pw_b = blocks(power)
        inv = [inv[g] + _dot(inv[g], pw_b[g]) for g in groups]
        v_b = blocks(v_g)
        z = [_dot(akmk[g], v_b[g]) for g in groups]
        akv = [z[g][:c] for g in groups]
        mrkv = [z[g][c:] for g in groups]
        at_b = blocks(at_g)
        akv_b = blocks(akv)
        a_hat = [_dot(inv[g], at_b[g]) for g in groups]
        u0 = [_dot(inv[g], akv_b[g]) for g in groups]
        ah_b = blocks(a_hat)
        u0_b = blocks(u0)
        r_hat = [rt_g[g] + _dot(m_rb[g], ah_b[g]) for g in groups]
        y0 = [_dot(m_rb[g], u0_b[g]) + mrkv[g] for g in groups]
        st = [st_ref[g] for g in groups]
        z = [_dot(stack(a_hat[g], r_hat[g]), st[g]) for g in groups]
        u = [z[g][:c] + u0[g] for g in groups]
        y = jnp.concatenate([z[g][c:] + y0[g] for g in groups], axis=1)
        for g in groups:
            bk = stack(bh[:, sls[g]], kh[:, sls[g]])
            uv = stack(u[g], v_g[g])
            dec_col = jnp.transpose(jnp.broadcast_to(dec[:, sls[g]], (LANE_TILE, gw)))
            st_ref[g] = (st[g] * jnp.concatenate([dec_col, dec_col], axis=1)
                         + jnp.where(same_head, _dot(bk, uv, TN), 0.0))
        yc = y - head_sums(y) * (1.0 / hd)
        var = head_sums(yc * yc) * (1.0 / hd)
        yn = yc * lax.rsqrt(var + GN_EPS) * lnw_ref[...] + lnb_ref[...]
        bonus = head_sums(r * k * rk_ref[...]) * v
        y_ref[rows, :] = yn + bonus
        return carry

    lax.fori_loop(0, chunks, chunk_body, 0)


def _rwkv_scan(r, ld, k, v, kku, a, r_k, ln_w, ln_b, *, batch, seq, tb=256):
    n, d = r.shape
    tpb = seq // tb
    blk = pl.BlockSpec((tb, d), lambda b, t: (b * tpb + t, 0))
    vec = _resident((1, d))
    return pl.pallas_call(
        functools.partial(_rwkv_scan_kernel, chunks=tb // CHUNK),
        grid=(batch, tpb),
        in_specs=[blk] * 6 + [vec] * 3,
        out_specs=blk,
        out_shape=jax.ShapeDtypeStruct((n, d), F32),
        scratch_shapes=[pltpu.VMEM((d // GROUP_W, GROUP_W, GROUP_W), F32)],
        compiler_params=_params(("parallel", "arbitrary")),
        name="rwkv_scan",
    )(r, ld, k, v, kku, a, r_k.reshape(1, d), ln_w.reshape(1, d), ln_b.reshape(1, d))


def _gated_out_kernel(y_ref, gate_ref, h_ref, wo_ref, o_ref):
    o_ref[...] = h_ref[...] + jnp.dot((y_ref[...] * gate_ref[...]).astype(BF16), wo_ref[...],
                                      preferred_element_type=F32)


def _gated_out(y, gate, h, w_o, *, tm=512):
    n, d = h.shape
    row = pl.BlockSpec((tm, d), lambda i: (i, 0))
    return pl.pallas_call(
        _gated_out_kernel,
        grid=(n // tm,),
        in_specs=[row, row, row, _resident(w_o.shape)],
        out_specs=row,
        out_shape=jax.ShapeDtypeStruct((n, d), F32),
        compiler_params=_params(("parallel",)),
        name="rwkv_out",
    )(y, gate, h, w_o)


def kernel(x, norm_mix_g, norm_ffn_g, final_norm_g, ffn_w_in, ffn_w_out, attn_w_qkv, attn_w_o, attn_sinks, pool_w, pool_scale, rwkv_mu, rwkv_w_rkv, rwkv_w0, rwkv_w1, rwkv_w2, rwkv_a0, rwkv_a1, rwkv_a2, rwkv_g1, rwkv_g2, rwkv_k_k, rwkv_k_a, rwkv_r_k, rwkv_ln_w, rwkv_ln_b, rwkv_w_o):
    batch, seq, d = x.shape
    depth = norm_mix_g.shape[0]
    n_mixers = 3
    bf = lambda w: w.astype(BF16)
    h = x.reshape(batch * seq, d)
    for i in range(depth):
        kind, j = i % n_mixers, i // n_mixers
        g = norm_mix_g[i]
        if kind == 0:
            q, k, v = _qkv(h, g, bf(attn_w_qkv[j]))
            h = _attention(h, q, k, v, attn_sinks[j], bf(attn_w_o[j]), seq=seq)
        elif kind == 1:
            h = _pool(h, g, bf(pool_w[j]), pool_scale[j], seq=seq)
        else:
            r, ld, k, v, kku, a, gate = _rwkv_proj(
                h, g, rwkv_mu[j], bf(rwkv_w_rkv[j]), rwkv_w0[j], bf(rwkv_w1[j]), bf(rwkv_w2[j]), rwkv_a0[j],
                bf(rwkv_a1[j]), bf(rwkv_a2[j]), bf(rwkv_g1[j]), bf(rwkv_g2[j]), rwkv_k_k[j], rwkv_k_a[j], seq=seq)
            y = _rwkv_scan(r, ld, k, v, kku, a, rwkv_r_k[j], rwkv_ln_w[j], rwkv_ln_b[j], batch=batch, seq=seq)
            h = _gated_out(y, gate, h, bf(rwkv_w_o[j]))
        h = _ffn(h, norm_ffn_g[i], bf(ffn_w_in[i]), bf(ffn_w_out[i]), final_norm_g, final=(i == depth - 1))
    return h.reshape(batch, seq, d)
```

```python
import functools

import jax
import jax.numpy as jnp
from jax import lax
from jax.experimental import pallas as pl
from jax.experimental.pallas import tpu as pltpu

F32 = jnp.float32
BF16 = jnp.bfloat16

RMS_EPS = 1e-6
GN_EPS = 64e-5
ATTN_HEAD_DIM = 64
ATTN_Q_HEADS = 16
ATTN_KV_HEADS = 4
ATTN_GROUP = ATTN_Q_HEADS // ATTN_KV_HEADS
ATTN_BLOCK = 128
POOL_WINDOWS = (2, 4, 8, 16)
POOL_HALO = 16
RWKV_HEAD_DIM = 64
CHUNK = 64
LANE_TILE = 128

VMEM_LIMIT_BYTES = 56 * 1024 * 1024

NN = (((1,), (0,)), ((), ()))
NT = (((1,), (1,)), ((), ()))
TN = (((0,), (0,)), ((), ()))


def _dot(a, b, dims=NN):
    return lax.dot_general(a.astype(BF16), b.astype(BF16), dims, preferred_element_type=F32)


def _rms(x, g):
    return x * lax.rsqrt(jnp.mean(x * x, axis=-1, keepdims=True) + RMS_EPS) * g


def _params(sem):
    return pltpu.CompilerParams(dimension_semantics=sem, vmem_limit_bytes=VMEM_LIMIT_BYTES)


def _resident(shape):
    nd = len(shape)
    return pl.BlockSpec(shape, lambda *_: (0,) * nd, pipeline_mode=pl.Buffered(1))


def _ffn_kernel(h_ref, g_ref, win_ref, wout_ref, gf_ref, o_ref, act_ref, *, hidden, tf, final):
    h = h_ref[...]
    xn = _rms(h, g_ref[...]).astype(BF16)
    for j in range(hidden // tf):
        gate = jnp.dot(xn, win_ref[:, j * tf:(j + 1) * tf], preferred_element_type=F32)
        up = jnp.dot(xn, win_ref[:, hidden + j * tf:hidden + (j + 1) * tf], preferred_element_type=F32)
        act_ref[:, j * tf:(j + 1) * tf] = (gate * jax.nn.sigmoid(gate) * up).astype(BF16)
    out = h + jnp.dot(act_ref[...], wout_ref[...], preferred_element_type=F32)
    if final:
        out = _rms(out, gf_ref[...])
    o_ref[...] = out


def _ffn(h, g, w_in, w_out, g_final, *, final, tm=512, tf=256):
    n, d = h.shape
    hidden = w_out.shape[0]
    row = pl.BlockSpec((tm, d), lambda i: (i, 0))
    return pl.pallas_call(
        functools.partial(_ffn_kernel, hidden=hidden, tf=tf, final=final),
        grid=(n // tm,),
        in_specs=[row, _resident((1, d)), _resident(w_in.shape), _resident(w_out.shape), _resident((1, d))],
        out_specs=row,
        out_shape=jax.ShapeDtypeStruct((n, d), F32),
        scratch_shapes=[pltpu.VMEM((tm, hidden), BF16)],
        compiler_params=_params(("parallel",)),
        name="ffn",
    )(h, g.reshape(1, d), w_in, w_out, g_final.reshape(1, d))


def _half_placed_tiles(x, lo, mask):
    tile = lambda a, t: a[:, t * LANE_TILE:(t + 1) * LANE_TILE]
    rolled = pltpu.roll(x, ATTN_HEAD_DIM, axis=1)
    out = []
    for g in range(ATTN_KV_HEADS):
        natural = tile(x, g // 2)
        moved = tile(rolled, ((g + 1) % ATTN_KV_HEADS) // 2)
        low, high = (natural, moved) if g % 2 == 0 else (moved, natural)
        if mask:
            low = jnp.where(lo, low, 0.0)
            high = jnp.where(lo, 0.0, high)
        out += [low, high]
    return out


def _qkv_kernel(h_ref, g_ref, w_ref, q_ref, kx_ref, vx_ref, *, qd, kvd, scale):
    xn = _rms(h_ref[...], g_ref[...]).astype(BF16)
    qkv = jnp.dot(xn, w_ref[...], preferred_element_type=F32)
    q_ref[...] = (qkv[:, :qd] * scale).astype(BF16)
    lo = lax.broadcasted_iota(jnp.int32, (qkv.shape[0], LANE_TILE), 1) < ATTN_HEAD_DIM
    for i, t in enumerate(_half_placed_tiles(qkv[:, qd:qd + kvd], lo, True)):
        kx_ref[:, i * LANE_TILE:(i + 1) * LANE_TILE] = t.astype(BF16)
    for i, t in enumerate(_half_placed_tiles(qkv[:, qd + kvd:], lo, False)):
        vx_ref[:, i * LANE_TILE:(i + 1) * LANE_TILE] = t.astype(BF16)


def _qkv(h, g, w_qkv, *, tm=512):
    n, d = h.shape
    qd = ATTN_Q_HEADS * ATTN_HEAD_DIM
    kvd = ATTN_KV_HEADS * ATTN_HEAD_DIM
    xd = 2 * ATTN_KV_HEADS * LANE_TILE
    return pl.pallas_call(
        functools.partial(_qkv_kernel, qd=qd, kvd=kvd, scale=ATTN_HEAD_DIM ** -0.5),
        grid=(n // tm,),
        in_specs=[pl.BlockSpec((tm, d), lambda i: (i, 0)), _resident((1, d)), _resident(w_qkv.shape)],
        out_specs=[pl.BlockSpec((tm, qd), lambda i: (i, 0)),
                   pl.BlockSpec((tm, xd), lambda i: (i, 0)),
                   pl.BlockSpec((tm, xd), lambda i: (i, 0))],
        out_shape=[jax.ShapeDtypeStruct((n, qd), BF16),
                   jax.ShapeDtypeStruct((n, xd), BF16),
                   jax.ShapeDtypeStruct((n, xd), BF16)],
        compiler_params=_params(("parallel",)),
        name="attn_qkv",
    )(h, g.reshape(1, d), w_qkv)


def _attn_kernel(q_ref, kp_ref, kc_ref, vp_ref, vc_ref, sink_ref, h_ref, wo_ref, o_ref, oh_ref, *, blocks_per_seq):
    blk = ATTN_BLOCK
    first = (pl.program_id(0) % blocks_per_seq) == 0
    qpos = lax.broadcasted_iota(jnp.int32, (2 * blk, 2 * blk), 0) % blk + blk
    kpos = lax.broadcasted_iota(jnp.int32, (2 * blk, 2 * blk), 1)
    diff = qpos - kpos
    valid = (diff >= 0) & (diff < blk) & (jnp.logical_not(first) | (kpos >= blk))
    lo = lax.broadcasted_iota(jnp.int32, (blk, LANE_TILE), 1) < ATTN_HEAD_DIM
    tile = lambda ref, t: ref[:, t * LANE_TILE:(t + 1) * LANE_TILE]
    band = lambda p_ref, c_ref, t: jnp.concatenate([tile(p_ref, t), tile(c_ref, t)], axis=0)
    kv_heads = range(ATTN_KV_HEADS)
    qa = [jnp.concatenate([tile(q_ref, 2 * g), tile(q_ref, 2 * g + 1)], axis=0) for g in kv_heads]

    items = [(g, half) for g in kv_heads for half in range(2)]
    col = lambda hq: jnp.broadcast_to(sink_ref[:, hq:hq + 1], (blk, 1))
    sink = [jnp.concatenate([col(4 * g + half), col(4 * g + 2 + half)], axis=0) for g, half in items]
    s = [lax.dot_general(qa[g], band(kp_ref, kc_ref, 2 * g + half), NT, preferred_element_type=F32)
         for g, half in items]
    s = [jnp.where(valid, x, -jnp.inf) for x in s]
    m = [jnp.maximum(jnp.max(x, axis=-1, keepdims=True), sk) for x, sk in zip(s, sink)]
    p = [jnp.exp(x - mx) for x, mx in zip(s, m)]
    den = [jnp.sum(x, axis=-1, keepdims=True) + jnp.exp(sk - mx) for x, sk, mx in zip(p, sink, m)]
    o = [jnp.dot(x.astype(BF16), band(vp_ref, vc_ref, 2 * g + half), preferred_element_type=F32)
         for x, (g, half) in zip(p, items)]
    o = [x / dn for x, dn in zip(o, den)]
    for g in kv_heads:
        for j in range(2):
            rows = slice(j * blk, (j + 1) * blk)
            oh_ref[:, (2 * g + j) * LANE_TILE:(2 * g + j + 1) * LANE_TILE] = jnp.where(
                lo, o[2 * g][rows], o[2 * g + 1][rows]).astype(BF16)
    o_ref[...] = h_ref[...] + jnp.dot(oh_ref[...], wo_ref[...], preferred_element_type=F32)


def _attention(h, q, kx, vx, sinks, w_o, *, seq):
    n, d = h.shape
    blk = ATTN_BLOCK
    qd = q.shape[1]
    xd = kx.shape[1]
    bps = seq // blk
    cur = lambda i: (i, 0)
    prev = lambda i: (jnp.maximum(i - 1, 0), 0)
    return pl.pallas_call(
        functools.partial(_attn_kernel, blocks_per_seq=bps),
        grid=(n // blk,),
        in_specs=[pl.BlockSpec((blk, qd), cur),
                  pl.BlockSpec((blk, xd), prev), pl.BlockSpec((blk, xd), cur),
                  pl.BlockSpec((blk, xd), prev), pl.BlockSpec((blk, xd), cur),
                  _resident((1, ATTN_Q_HEADS)),
                  pl.BlockSpec((blk, d), cur),
                  _resident(w_o.shape)],
        out_specs=pl.BlockSpec((blk, d), cur),
        out_shape=jax.ShapeDtypeStruct((n, d), F32),
        scratch_shapes=[pltpu.VMEM((blk, qd), BF16)],
        compiler_params=_params(("parallel",)),
        name="attn_core",
    )(q, kx, kx, vx, vx, sinks.reshape(1, -1), h, w_o)


def _pool_kernel(hp_ref, hc_ref, g_ref, pw_ref, ps_ref, o_ref, *, tiles_per_seq, tm):
    halo = POOL_HALO
    gdim = pw_ref.shape[1]
    tile = pl.program_id(0) % tiles_per_seq
    h = hc_ref[...]
    g = g_ref[...]
    keep = (tile > 0).astype(F32)
    ext = jnp.concatenate([_rms(hp_ref[...], g) * keep, _rms(h, g)], axis=0)
    pos = tile * tm + lax.broadcasted_iota(jnp.int32, (tm, 1), 0)
    for gi, w in enumerate(POOL_WINDOWS):
        csl = slice(gi * gdim, (gi + 1) * gdim)
        x = ext[:, csl]
        s = x
        span = 1
        while span < w:
            s = s + pltpu.roll(s, span, axis=0)
            span *= 2
        cnt = jnp.minimum(pos + 1, w).astype(F32)
        dpool = s[halo:, :] / cnt - x[halo:, :]
        y = jnp.dot(dpool.astype(BF16), pw_ref[gi], preferred_element_type=F32)
        o_ref[:, csl] = h[:, csl] + y * ps_ref[:, csl]


def _pool(h, g, pool_w, pool_scale, *, seq, tm=512):
    n, d = h.shape
    halo = POOL_HALO
    per = tm // halo
    return pl.pallas_call(
        functools.partial(_pool_kernel, tiles_per_seq=seq // tm, tm=tm),
        grid=(n // tm,),
        in_specs=[pl.BlockSpec((halo, d), lambda i: (jnp.maximum(i * per - 1, 0), 0)),
                  pl.BlockSpec((tm, d), lambda i: (i, 0)),
                  _resident((1, d)), _resident(pool_w.shape), _resident((1, d))],
        out_specs=pl.BlockSpec((tm, d), lambda i: (i, 0)),
        out_shape=jax.ShapeDtypeStruct((n, d), F32),
        compiler_params=_params(("parallel",)),
        name="pool_mix",
    )(h, h, g.reshape(1, d), pool_w, pool_scale.reshape(1, d))


def _rwkv_proj_kernel(hp_ref, hc_ref, g_ref, mu_ref, wrkv_ref, w0_ref, w1_ref, w2_ref, a0_ref, a1_ref, a2_ref,
                      g1_ref, g2_ref, kk_ref, ka_ref,
                      r_ref, ld_ref, k_ref, v_ref, kku_ref, a_ref, gate_ref, *, tiles_per_seq, tm):
    tile = pl.program_id(0) % tiles_per_seq
    g = g_ref[...]
    xn = _rms(hc_ref[...], g)
    last = _rms(hp_ref[...], g)[-1:, :] * (tile > 0).astype(F32)
    rowid = lax.broadcasted_iota(jnp.int32, (tm, 1), 0)
    shifted = jnp.where(rowid == 0, last, pltpu.roll(xn, 1, axis=0))
    dx = shifted - xn

    def mix(c):
        return (xn + dx * mu_ref[c:c + 1, :]).astype(BF16)

    r = jnp.dot(mix(0), wrkv_ref[0], preferred_element_type=F32)
    k = jnp.dot(mix(1), wrkv_ref[1], preferred_element_type=F32)
    v = jnp.dot(mix(2), wrkv_ref[2], preferred_element_type=F32)
    wl = w0_ref[...] + _dot(jnp.tanh(jnp.dot(mix(3), w1_ref[...], preferred_element_type=F32)), w2_ref[...])
    w_log = -jax.nn.softplus(-wl) - 0.5
    a = jax.nn.sigmoid(a0_ref[...] + _dot(jnp.dot(mix(4), a1_ref[...], preferred_element_type=F32), a2_ref[...]))
    gate = _dot(jax.nn.sigmoid(jnp.dot(mix(5), g1_ref[...], preferred_element_type=F32)), g2_ref[...])
    r_ref[...] = r
    ld_ref[...] = -jnp.exp(w_log)
    k_ref[...] = k * (1.0 + (a - 1.0) * ka_ref[...])
    v_ref[...] = v
    kku_ref[...] = k * kk_ref[...]
    a_ref[...] = a
    gate_ref[...] = gate


def _rwkv_proj(h, g, mu, w_rkv, w0, w1, w2, a0, a1, a2, g1, g2, k_k, k_a, *, seq, tm=512):
    n, d = h.shape
    row = pl.BlockSpec((tm, d), lambda i: (i, 0))
    per = tm // 8
    vec = lambda x: x.reshape(1, d)
    outs = [jax.ShapeDtypeStruct((n, d), F32)] * 7
    return pl.pallas_call(
        functools.partial(_rwkv_proj_kernel, tiles_per_seq=seq // tm, tm=tm),
        grid=(n // tm,),
        in_specs=[pl.BlockSpec((8, d), lambda i: (jnp.maximum(i * per - 1, 0), 0)), row,
                  _resident((1, d)), _resident(mu.shape), _resident(w_rkv.shape),
                  _resident((1, d)), _resident(w1.shape), _resident(w2.shape),
                  _resident((1, d)), _resident(a1.shape), _resident(a2.shape),
                  _resident(g1.shape), _resident(g2.shape), _resident((1, d)), _resident((1, d))],
        out_specs=[row] * 7,
        out_shape=outs,
        compiler_params=_params(("parallel",)),
        name="rwkv_proj",
    )(h, h, vec(g), mu, w_rkv, vec(w0), w1, w2, vec(a0), a1, a2, g1, g2, vec(k_k), vec(k_a))


HEAD_GROUP = 4
GROUP_W = HEAD_GROUP * RWKV_HEAD_DIM
assert CHUNK == RWKV_HEAD_DIM and GROUP_W == 2 * LANE_TILE


def _block_rows(x, lo):
    x0, x1 = x[:, :LANE_TILE], x[:, LANE_TILE:]
    zf = jnp.zeros_like(x0)
    z = jnp.zeros(x0.shape, BF16)
    keep = lambda m, t: jnp.where(m, t, zf).astype(BF16)
    hi = jnp.logical_not(lo)
    return jnp.concatenate([
        jnp.concatenate([keep(lo, x0), z], axis=1),
        jnp.concatenate([keep(hi, x0), z], axis=1),
        jnp.concatenate([z, keep(lo, x1)], axis=1),
        jnp.concatenate([z, keep(hi, x1)], axis=1)], axis=0)


def _cumsum_rows(tri_incl_bf16, x):
    hi = x.astype(BF16)
    rem = x - hi.astype(F32)
    mid = rem.astype(BF16)
    lo = (rem - mid.astype(F32)).astype(BF16)
    f = lambda z: jnp.dot(tri_incl_bf16, z, preferred_element_type=F32)
    return f(hi) + f(mid) + f(lo)


def _rwkv_scan_kernel(r_ref, ld_ref, k_ref, v_ref, kku_ref, a_ref, rk_ref, lnw_ref, lnb_ref, y_ref, st_ref,
                      *, chunks):
    hd = RWKV_HEAD_DIM
    c = CHUNK
    gw = GROUP_W
    d = r_ref.shape[1]

    @pl.when(pl.program_id(1) == 0)
    def _():
        st_ref[...] = jnp.zeros_like(st_ref)

    t_idx = lax.broadcasted_iota(jnp.int32, (c, gw), 0)
    s_idx = lax.broadcasted_iota(jnp.int32, (c, gw), 1) % c
    tri_strict = t_idx > s_idx
    tri_incl = t_idx >= s_idx
    eye = (t_idx == s_idx).astype(F32)
    lo = lax.broadcasted_iota(jnp.int32, (c, LANE_TILE), 1) < hd
    same_head = (lax.broadcasted_iota(jnp.int32, (gw, gw), 0) // hd
                 == lax.broadcasted_iota(jnp.int32, (gw, gw), 1) // hd)
    tri_c = (lax.broadcasted_iota(jnp.int32, (c, c), 0) >= lax.broadcasted_iota(jnp.int32, (c, c), 1)).astype(BF16)

    def head_sums(x):
        tiles = []
        for t in range(d // LANE_TILE):
            xt = x[:, t * LANE_TILE:(t + 1) * LANE_TILE]
            s0 = jnp.sum(jnp.where(lo, xt, 0.0), axis=-1, keepdims=True)
            s1 = jnp.sum(jnp.where(lo, 0.0, xt), axis=-1, keepdims=True)
            tiles.append(jnp.where(lo, s0, s1))
        return jnp.concatenate(tiles, axis=1)

    def chunk_body(ci, carry):
        rows = pl.ds(pl.multiple_of(ci * c, c), c)
        r = r_ref[rows, :]
        ld = ld_ref[rows, :]
        k = k_ref[rows, :]
        v = v_ref[rows, :]
        kku = kku_ref[rows, :]
        lr = a_ref[rows, :]
        kk = kku / jnp.maximum(jnp.sqrt(head_sums(kku * kku)), 1e-12)
        cum = _cumsum_rows(tri_c, ld)
        total = cum[c - 1:c, :]
        b = kk * lr
        e_neg = jnp.exp(-cum)
        e_tot = jnp.exp(total - cum)
        at = -kk * jnp.exp(cum - ld)
        bt = b * e_neg
        kt = k * e_neg
        rt = r * jnp.exp(cum)
        bh = b * e_tot
        kh = k * e_tot
        dec = jnp.exp(total)
        groups = range(d // gw)
        sls = [slice(g * gw, (g + 1) * gw) for g in groups]
        stack = lambda top, bottom: jnp.concatenate([top, bottom], axis=0)
        blocks = lambda xs: [_block_rows(x, lo) for x in xs]
        at_g = [at[:, s] for s in sls]
        rt_g = [rt[:, s] for s in sls]
        v_g = [v[:, s] for s in sls]
        ar = [stack(at_g[g], rt_g[g]).astype(BF16) for g in groups]
        bt_b = blocks([bt[:, s] for s in sls])
        kt_b = blocks([kt[:, s] for s in sls])
        ab = [_dot(ar[g], bt_b[g], NT) for g in groups]
        ak = [_dot(ar[g], kt_b[g], NT) for g in groups]
        a_ab = [jnp.where(tri_strict, ab[g][:c], 0.0) for g in groups]
        m_rb = [jnp.where(tri_incl, ab[g][c:], 0.0) for g in groups]
        akmk = [stack(jnp.where(tri_strict, ak[g][:c], 0.0), jnp.where(tri_incl, ak[g][c:], 0.0)) for g in groups]
        inv = [eye + a_ab[g] for g in groups]
        pw_b = blocks(a_ab)
        power = [_dot(a_ab[g], pw_b[g]) for g in groups]
        for _ in range(4):
            pw_b = blocks(power)
            z = [_dot(stack(inv[g], power[g]), pw_b[g]) for g in groups]
            inv = [inv[g] + z[g][:c] for g in groups]
            power = [z[g][c:] for g in groups]
        pw_b = blocks(power)
        inv = [inv[g] + _dot(inv[g], pw_b[g]) for g in groups]
        v_b = blocks(v_g)
        z = [_dot(akmk[g], v_b[g]) for g in groups]
        akv = [z[g][:c] for g in groups]
        mrkv = [z[g][c:] for g in groups]
        at_b = blocks(at_g)
        akv_b = blocks(akv)
        a_hat = [_dot(inv[g], at_b[g]) for g in groups]
        u0 = [_dot(inv[g], akv_b[g]) for g in groups]
        ah_b = blocks(a_hat)
        u0_b = blocks(u0)
        r_hat = [rt_g[g] + _dot(m_rb[g], ah_b[g]) for g in groups]
        y0 = [_dot(m_rb[g], u0_b[g]) + mrkv[g] for g in groups]
        st = [st_ref[g] for g in groups]
        z = [_dot(stack(a_hat[g], r_hat[g]), st[g]) for g in groups]
        u = [z[g][:c] + u0[g] for g in groups]
        y = jnp.concatenate([z[g][c:] + y0[g] for g in groups], axis=1)
        for g in groups:
            bk = stack(bh[:, sls[g]], kh[:, sls[g]])
            uv = stack(u[g], v_g[g])
            dec_col = jnp.transpose(jnp.broadcast_to(dec[:, sls[g]], (LANE_TILE, gw)))
            st_ref[g] = (st[g] * jnp.concatenate([dec_col, dec_col], axis=1)
                         + jnp.where(same_head, _dot(bk, uv, TN), 0.0))
        yc = y - head_sums(y) * (1.0 / hd)
        var = head_sums(yc * yc) * (1.0 / hd)
        yn = yc * lax.rsqrt(var + GN_EPS) * lnw_ref[...] + lnb_ref[...]
        bonus = head_sums(r * k * rk_ref[...]) * v
        y_ref[rows, :] = yn + bonus
        return carry

    lax.fori_loop(0, chunks, chunk_body, 0)


def _rwkv_scan(r, ld, k, v, kku, a, r_k, ln_w, ln_b, *, batch, seq, tb=256):
    n, d = r.shape
    tpb = seq // tb
    blk = pl.BlockSpec((tb, d), lambda b, t: (b * tpb + t, 0))
    vec = _resident((1, d))
    return pl.pallas_call(
        functools.partial(_rwkv_scan_kernel, chunks=tb // CHUNK),
        grid=(batch, tpb),
        in_specs=[blk] * 6 + [vec] * 3,
        out_specs=blk,
        out_shape=jax.ShapeDtypeStruct((n, d), F32),
        scratch_shapes=[pltpu.VMEM((d // GROUP_W, GROUP_W, GROUP_W), F32)],
        compiler_params=_params(("parallel", "arbitrary")),
        name="rwkv_scan",
    )(r, ld, k, v, kku, a, r_k.reshape(1, d), ln_w.reshape(1, d), ln_b.reshape(1, d))


def _gated_out_kernel(y_ref, gate_ref, h_ref, wo_ref, o_ref):
    o_ref[...] = h_ref[...] + jnp.dot((y_ref[...] * gate_ref[...]).astype(BF16), wo_ref[...],
                                      preferred_element_type=F32)


def _gated_out(y, gate, h, w_o, *, tm=512):
    n, d = h.shape
    row = pl.BlockSpec((tm, d), lambda i: (i, 0))
    return pl.pallas_call(
        _gated_out_kernel,
        grid=(n // tm,),
        in_specs=[row, row, row, _resident(w_o.shape)],
        out_specs=row,
        out_shape=jax.ShapeDtypeStruct((n, d), F32),
        compiler_params=_params(("parallel",)),
        name="rwkv_out",
    )(y, gate, h, w_o)


def kernel(x, norm_mix_g, norm_ffn_g, final_norm_g, ffn_w_in, ffn_w_out, attn_w_qkv, attn_w_o, attn_sinks, pool_w, pool_scale, rwkv_mu, rwkv_w_rkv, rwkv_w0, rwkv_w1, rwkv_w2, rwkv_a0, rwkv_a1, rwkv_a2, rwkv_g1, rwkv_g2, rwkv_k_k, rwkv_k_a, rwkv_r_k, rwkv_ln_w, rwkv_ln_b, rwkv_w_o):
    batch, seq, d = x.shape
    depth = norm_mix_g.shape[0]
    n_mixers = 3
    bf = lambda w: w.astype(BF16)
    h = x.reshape(batch * seq, d)
    for i in range(depth):
        kind, j = i % n_mixers, i // n_mixers
        g = norm_mix_g[i]
        if kind == 0:
            q, k, v = _qkv(h, g, bf(attn_w_qkv[j]))
            h = _attention(h, q, k, v, attn_sinks[j], bf(attn_w_o[j]), seq=seq)
        elif kind == 1:
            h = _pool(h, g, bf(pool_w[j]), pool_scale[j], seq=seq)
        else:
            r, ld, k, v, kku, a, gate = _rwkv_proj(
                h, g, rwkv_mu[j], bf(rwkv_w_rkv[j]), rwkv_w0[j], bf(rwkv_w1[j]), bf(rwkv_w2[j]), rwkv_a0[j],
                bf(rwkv_a1[j]), bf(rwkv_a2[j]), bf(rwkv_g1[j]), bf(rwkv_g2[j]), rwkv_k_k[j], rwkv_k_a[j], seq=seq)
            y = _rwkv_scan(r, ld, k, v, kku, a, rwkv_r_k[j], rwkv_ln_w[j], rwkv_ln_b[j], batch=batch, seq=seq)
            h = _gated_out(y, gate, h, bf(rwkv_w_o[j]))
        h = _ffn(h, norm_ffn_g[i], bf(ffn_w_in[i]), bf(ffn_w_out[i]), final_norm_g, final=(i == depth - 1))
    return h.reshape(batch, seq, d)
```

```python
import functools

import jax
import jax.numpy as jnp
from jax import lax
from jax.experimental import pallas as pl
from jax.experimental.pallas import tpu as pltpu

F32 = jnp.float32
BF16 = jnp.bfloat16

RMS_EPS = 1e-6
GN_EPS = 64e-5
ATTN_HEAD_DIM = 64
ATTN_Q_HEADS = 16
ATTN_KV_HEADS = 4
ATTN_GROUP = ATTN_Q_HEADS // ATTN_KV_HEADS
ATTN_BLOCK = 128
POOL_WINDOWS = (2, 4, 8, 16)
POOL_HALO = 16
RWKV_HEAD_DIM = 64
CHUNK = 64
LANE_TILE = 128

VMEM_LIMIT_BYTES = 56 * 1024 * 1024

NN = (((1,), (0,)), ((), ()))
NT = (((1,), (1,)), ((), ()))
TN = (((0,), (0,)), ((), ()))


def _dot(a, b, dims=NN):
    return lax.dot_general(a.astype(BF16), b.astype(BF16), dims, preferred_element_type=F32)


def _rms(x, g):
    return x * lax.rsqrt(jnp.mean(x * x, axis=-1, keepdims=True) + RMS_EPS) * g


def _params(sem):
    return pltpu.CompilerParams(dimension_semantics=sem, vmem_limit_bytes=VMEM_LIMIT_BYTES)


def _resident(shape):
    nd = len(shape)
    return pl.BlockSpec(shape, lambda *_: (0,) * nd, pipeline_mode=pl.Buffered(1))


def _ffn_kernel(h_ref, g_ref, win_ref, wout_ref, gf_ref, o_ref, act_ref, *, hidden, tf, final):
    h = h_ref[...]
    xn = _rms(h, g_ref[...]).astype(BF16)
    for j in range(hidden // tf):
        gate = jnp.dot(xn, win_ref[:, j * tf:(j + 1) * tf], preferred_element_type=F32)
        up = jnp.dot(xn, win_ref[:, hidden + j * tf:hidden + (j + 1) * tf], preferred_element_type=F32)
        act_ref[:, j * tf:(j + 1) * tf] = (gate * jax.nn.sigmoid(gate) * up).astype(BF16)
    out = h + jnp.dot(act_ref[...], wout_ref[...], preferred_element_type=F32)
    if final:
        out = _rms(out, gf_ref[...])
    o_ref[...] = out


def _ffn(h, g, w_in, w_out, g_final, *, final, tm=512, tf=256):
    n, d = h.shape
    hidden = w_out.shape[0]
    row = pl.BlockSpec((tm, d), lambda i: (i, 0))
    return pl.pallas_call(
        functools.partial(_ffn_kernel, hidden=hidden, tf=tf, final=final),
        grid=(n // tm,),
        in_specs=[row, _resident((1, d)), _resident(w_in.shape), _resident(w_out.shape), _resident((1, d))],
        out_specs=row,
        out_shape=jax.ShapeDtypeStruct((n, d), F32),
        scratch_shapes=[pltpu.VMEM((tm, hidden), BF16)],
        compiler_params=_params(("parallel",)),
        name="ffn",
    )(h, g.reshape(1, d), w_in, w_out, g_final.reshape(1, d))


def _half_placed_tiles(x, lo, mask):
    tile = lambda a, t: a[:, t * LANE_TILE:(t + 1) * LANE_TILE]
    rolled = pltpu.roll(x, ATTN_HEAD_DIM, axis=1)
    out = []
    for g in range(ATTN_KV_HEADS):
        natural = tile(x, g // 2)
        moved = tile(rolled, ((g + 1) % ATTN_KV_HEADS) // 2)
        low, high = (natural, moved) if g % 2 == 0 else (moved, natural)
        if mask:
            low = jnp.where(lo, low, 0.0)
            high = jnp.where(lo, 0.0, high)
        out += [low, high]
    return out


def _qkv_kernel(h_ref, g_ref, w_ref, q_ref, kx_ref, vxt_ref, *, qd, kvd, scale):
    xn = _rms(h_ref[...], g_ref[...]).astype(BF16)
    qkv = jnp.dot(xn, w_ref[...], preferred_element_type=F32)
    q_ref[...] = (qkv[:, :qd] * scale).astype(BF16)
    lo = lax.broadcasted_iota(jnp.int32, (qkv.shape[0], LANE_TILE), 1) < ATTN_HEAD_DIM
    for i, t in enumerate(_half_placed_tiles(qkv[:, qd:qd + kvd], lo, True)):
        kx_ref[:, i * LANE_TILE:(i + 1) * LANE_TILE] = t.astype(BF16)
    for i, t in enumerate(_half_placed_tiles(qkv[:, qd + kvd:], lo, False)):
        vxt_ref[i * LANE_TILE:(i + 1) * LANE_TILE, :] = t.T.astype(BF16)


def _qkv(h, g, w_qkv, *, tm=512):
    n, d = h.shape
    qd = ATTN_Q_HEADS * ATTN_HEAD_DIM
    kvd = ATTN_KV_HEADS * ATTN_HEAD_DIM
    xd = 2 * ATTN_KV_HEADS * LANE_TILE
    return pl.pallas_call(
        functools.partial(_qkv_kernel, qd=qd, kvd=kvd, scale=ATTN_HEAD_DIM ** -0.5),
        grid=(n // tm,),
        in_specs=[pl.BlockSpec((tm, d), lambda i: (i, 0)), _resident((1, d)), _resident(w_qkv.shape)],
        out_specs=[pl.BlockSpec((tm, qd), lambda i: (i, 0)),
                   pl.BlockSpec((tm, xd), lambda i: (i, 0)),
                   pl.BlockSpec((xd, tm), lambda i: (0, i))],
        out_shape=[jax.ShapeDtypeStruct((n, qd), BF16),
                   jax.ShapeDtypeStruct((n, xd), BF16),
                   jax.ShapeDtypeStruct((xd, n), BF16)],
        compiler_params=_params(("parallel",)),
        name="attn_qkv",
    )(h, g.reshape(1, d), w_qkv)


def _attn_kernel(q_ref, kp_ref, kc_ref, vtp_ref, vtc_ref, sink_ref, h_ref, wo_ref, o_ref, oht_ref,
                 *, blocks_per_seq):
    blk = ATTN_BLOCK
    hd = ATTN_HEAD_DIM
    first = (pl.program_id(0) % blocks_per_seq) == 0
    kpos = lax.broadcasted_iota(jnp.int32, (2 * blk, 2 * blk), 0)
    qpos = lax.broadcasted_iota(jnp.int32, (2 * blk, 2 * blk), 1) % blk + blk
    diff = qpos - kpos
    valid = (diff >= 0) & (diff < blk) & (jnp.logical_not(first) | (kpos >= blk))
    tile = lambda ref, t: ref[:, t * LANE_TILE:(t + 1) * LANE_TILE]
    kband = lambda t: jnp.concatenate([tile(kp_ref, t), tile(kc_ref, t)], axis=0)
    rows_t = lambda ref, t: ref[t * LANE_TILE:(t + 1) * LANE_TILE, :]
    vband_t = lambda t: jnp.concatenate([rows_t(vtp_ref, t), rows_t(vtc_ref, t)], axis=1)
    kv_heads = range(ATTN_KV_HEADS)
    qa = [jnp.concatenate([tile(q_ref, 2 * g), tile(q_ref, 2 * g + 1)], axis=0) for g in kv_heads]
    items = [(g, half) for g in kv_heads for half in range(2)]
    srow = lambda hq: jnp.broadcast_to(sink_ref[:, hq:hq + 1], (1, blk))
    sink = [jnp.concatenate([srow(4 * g + half), srow(4 * g + 2 + half)], axis=1) for g, half in items]
    s = [lax.dot_general(kband(2 * g + half), qa[g], NT, preferred_element_type=F32) for g, half in items]
    s = [jnp.where(valid, x, -jnp.inf) for x in s]
    m = [jnp.maximum(jnp.max(x, axis=0, keepdims=True), sk) for x, sk in zip(s, sink)]
    p = [jnp.exp(x - mx) for x, mx in zip(s, m)]
    den = [jnp.sum(x, axis=0, keepdims=True) + jnp.exp(sk - mx) for x, sk, mx in zip(p, sink, m)]
    o = [jnp.dot(vband_t(2 * g + half), x.astype(BF16), preferred_element_type=F32)
         for x, (g, half) in zip(p, items)]
    o = [x / dn for x, dn in zip(o, den)]
    for (g, half), x in zip(items, o):
        for j in range(2):
            hq = 4 * g + half + 2 * j
            oht_ref[hq * hd:(hq + 1) * hd, :] = x[half * hd:(half + 1) * hd, j * blk:(j + 1) * blk].astype(BF16)
    o_ref[...] = h_ref[...] + lax.dot_general(oht_ref[...], wo_ref[...], TN, preferred_element_type=F32)


def _attention(h, q, kx, vxt, sinks, w_o, *, seq):
    n, d = h.shape
    blk = ATTN_BLOCK
    qd = q.shape[1]
    xd = kx.shape[1]
    bps = seq // blk
    cur = lambda i: (i, 0)
    prev = lambda i: (jnp.maximum(i - 1, 0), 0)
    cur_t = lambda i: (0, i)
    prev_t = lambda i: (0, jnp.maximum(i - 1, 0))
    return pl.pallas_call(
        functools.partial(_attn_kernel, blocks_per_seq=bps),
        grid=(n // blk,),
        in_specs=[pl.BlockSpec((blk, qd), cur),
                  pl.BlockSpec((blk, xd), prev), pl.BlockSpec((blk, xd), cur),
                  pl.BlockSpec((xd, blk), prev_t), pl.BlockSpec((xd, blk), cur_t),
                  _resident((1, ATTN_Q_HEADS)),
                  pl.BlockSpec((blk, d), cur),
                  _resident(w_o.shape)],
        out_specs=pl.BlockSpec((blk, d), cur),
        out_shape=jax.ShapeDtypeStruct((n, d), F32),
        scratch_shapes=[pltpu.VMEM((qd, blk), BF16)],
        compiler_params=_params(("parallel",)),
        name="attn_core",
    )(q, kx, kx, vxt, vxt, sinks.reshape(1, -1), h, w_o)


def _pool_kernel(hp_ref, hc_ref, g_ref, pw_ref, ps_ref, o_ref, *, tiles_per_seq, tm):
    halo = POOL_HALO
    gdim = pw_ref.shape[1]
    tile = pl.program_id(0) % tiles_per_seq
    h = hc_ref[...]
    g = g_ref[...]
    keep = (tile > 0).astype(F32)
    ext = jnp.concatenate([_rms(hp_ref[...], g) * keep, _rms(h, g)], axis=0)
    pos = tile * tm + lax.broadcasted_iota(jnp.int32, (tm, 1), 0)
    for gi, w in enumerate(POOL_WINDOWS):
        csl = slice(gi * gdim, (gi + 1) * gdim)
        x = ext[:, csl]
        s = x
        span = 1
        while span < w:
            s = s + pltpu.roll(s, span, axis=0)
            span *= 2
        cnt = jnp.minimum(pos + 1, w).astype(F32)
        dpool = s[halo:, :] / cnt - x[halo:, :]
        y = jnp.dot(dpool.astype(BF16), pw_ref[gi], preferred_element_type=F32)
        o_ref[:, csl] = h[:, csl] + y * ps_ref[:, csl]


def _pool(h, g, pool_w, pool_scale, *, seq, tm=512):
    n, d = h.shape
    halo = POOL_HALO
    per = tm // halo
    return pl.pallas_call(
        functools.partial(_pool_kernel, tiles_per_seq=seq // tm, tm=tm),
        grid=(n // tm,),
        in_specs=[pl.BlockSpec((halo, d), lambda i: (jnp.maximum(i * per - 1, 0), 0)),
                  pl.BlockSpec((tm, d), lambda i: (i, 0)),
                  _resident((1, d)), _resident(pool_w.shape), _resident((1, d))],
        out_specs=pl.BlockSpec((tm, d), lambda i: (i, 0)),
        out_shape=jax.ShapeDtypeStruct((n, d), F32),
        compiler_params=_params(("parallel",)),
        name="pool_mix",
    )(h, h, g.reshape(1, d), pool_w, pool_scale.reshape(1, d))


def _rwkv_proj_kernel(hp_ref, hc_ref, g_ref, mu_ref, wrkv_ref, w0_ref, w1_ref, w2_ref, a0_ref, a1_ref, a2_ref,
                      g1_ref, g2_ref, kk_ref, ka_ref,
                      r_ref, ld_ref, k_ref, v_ref, kku_ref, a_ref, gate_ref, *, tiles_per_seq, tm):
    tile = pl.program_id(0) % tiles_per_seq
    g = g_ref[...]
    xn = _rms(hc_ref[...], g)
    last = _rms(hp_ref[...], g)[-1:, :] * (tile > 0).astype(F32)
    rowid = lax.broadcasted_iota(jnp.int32, (tm, 1), 0)
    shifted = jnp.where(rowid == 0, last, pltpu.roll(xn, 1, axis=0))
    dx = shifted - xn

    def mix(c):
        return (xn + dx * mu_ref[c:c + 1, :]).astype(BF16)

    r = jnp.dot(mix(0), wrkv_ref[0], preferred_element_type=F32)
    k = jnp.dot(mix(1), wrkv_ref[1], preferred_element_type=F32)
    v = jnp.dot(mix(2), wrkv_ref[2], preferred_element_type=F32)
    wl = w0_ref[...] + _dot(jnp.tanh(jnp.dot(mix(3), w1_ref[...], preferred_element_type=F32)), w2_ref[...])
    w_log = -jax.nn.softplus(-wl) - 0.5
    a = jax.nn.sigmoid(a0_ref[...] + _dot(jnp.dot(mix(4), a1_ref[...], preferred_element_type=F32), a2_ref[...]))
    gate = _dot(jax.nn.sigmoid(jnp.dot(mix(5), g1_ref[...], preferred_element_type=F32)), g2_ref[...])
    r_ref[...] = r
    ld_ref[...] = -jnp.exp(w_log)
    k_ref[...] = k * (1.0 + (a - 1.0) * ka_ref[...])
    v_ref[...] = v
    kku_ref[...] = k * kk_ref[...]
    a_ref[...] = a
    gate_ref[...] = gate


def _rwkv_proj(h, g, mu, w_rkv, w0, w1, w2, a0, a1, a2, g1, g2, k_k, k_a, *, seq, tm=512):
    n, d = h.shape
    row = pl.BlockSpec((tm, d), lambda i: (i, 0))
    per = tm // 8
    vec = lambda x: x.reshape(1, d)
    outs = [jax.ShapeDtypeStruct((n, d), F32)] * 7
    return pl.pallas_call(
        functools.partial(_rwkv_proj_kernel, tiles_per_seq=seq // tm, tm=tm),
        grid=(n // tm,),
        in_specs=[pl.BlockSpec((8, d), lambda i: (jnp.maximum(i * per - 1, 0), 0)), row,
                  _resident((1, d)), _resident(mu.shape), _resident(w_rkv.shape),
                  _resident((1, d)), _resident(w1.shape), _resident(w2.shape),
                  _resident((1, d)), _resident(a1.shape), _resident(a2.shape),
                  _resident(g1.shape), _resident(g2.shape), _resident((1, d)), _resident((1, d))],
        out_specs=[row] * 7,
        out_shape=outs,
        compiler_params=_params(("parallel",)),
        name="rwkv_proj",
    )(h, h, vec(g), mu, w_rkv, vec(w0), w1, w2, vec(a0), a1, a2, g1, g2, vec(k_k), vec(k_a))


HEAD_GROUP = 4
GROUP_W = HEAD_GROUP * RWKV_HEAD_DIM
assert CHUNK == RWKV_HEAD_DIM and GROUP_W == 2 * LANE_TILE


def _block_rows(x, lo):
    x0, x1 = x[:, :LANE_TILE], x[:, LANE_TILE:]
    zf = jnp.zeros_like(x0)
    z = jnp.zeros(x0.shape, BF16)
    keep = lambda m, t: jnp.where(m, t, zf).astype(BF16)
    hi = jnp.logical_not(lo)
    return jnp.concatenate([
        jnp.concatenate([keep(lo, x0), z], axis=1),
        jnp.concatenate([keep(hi, x0), z], axis=1),
        jnp.concatenate([z, keep(lo, x1)], axis=1),
        jnp.concatenate([z, keep(hi, x1)], axis=1)], axis=0)


def _cumsum_rows(tri_incl_bf16, x):
    hi = x.astype(BF16)
    rem = x - hi.astype(F32)
    mid = rem.astype(BF16)
    lo = (rem - mid.astype(F32)).astype(BF16)
    f = lambda z: jnp.dot(tri_incl_bf16, z, preferred_element_type=F32)
    return f(hi) + f(mid) + f(lo)


def _rwkv_scan_kernel(r_ref, ld_ref, k_ref, v_ref, kku_ref, a_ref, rk_ref, lnw_ref, lnb_ref, y_ref, st_ref,
                      *, chunks):
    hd = RWKV_HEAD_DIM
    c = CHUNK
    gw = GROUP_W
    nb, _, d = r_ref.shape

    @pl.when(pl.program_id(1) == 0)
    def _():
        st_ref[...] = jnp.zeros_like(st_ref)

    t_idx = lax.broadcasted_iota(jnp.int32, (c, gw), 0)
    s_idx = lax.broadcasted_iota(jnp.int32, (c, gw), 1) % c
    tri_strict = t_idx > s_idx
    tri_incl = t_idx >= s_idx
    eye = (t_idx == s_idx).astype(F32)
    lo = lax.broadcasted_iota(jnp.int32, (nb * c, LANE_TILE), 1) < hd
    lo_c = lax.broadcasted_iota(jnp.int32, (c, LANE_TILE), 1) < hd
    same_head = (lax.broadcasted_iota(jnp.int32, (gw, gw), 0) // hd
                 == lax.broadcasted_iota(jnp.int32, (gw, gw), 1) // hd)
    row_c = lax.broadcasted_iota(jnp.int32, (nb * c, nb * c), 0)
    col_c = lax.broadcasted_iota(jnp.int32, (nb * c, nb * c), 1)
    tri_c = ((row_c >= col_c) & (row_c // c == col_c // c)).astype(BF16)

    def head_sums(x):
        tiles = []
        for t in range(d // LANE_TILE):
            xt = x[:, t * LANE_TILE:(t + 1) * LANE_TILE]
            s0 = jnp.sum(jnp.where(lo, xt, 0.0), axis=-1, keepdims=True)
            s1 = jnp.sum(jnp.where(lo, 0.0, xt), axis=-1, keepdims=True)
            tiles.append(jnp.where(lo, s0, s1))
        return jnp.concatenate(tiles, axis=1)

    def chunk_body(ci, carry):
        rows = pl.ds(pl.multiple_of(ci * c, c), c)
        load = lambda ref: jnp.concatenate([ref[b, rows, :] for b in range(nb)], axis=0)
        r = load(r_ref)
        ld = load(ld_ref)
        k = load(k_ref)
        v = load(v_ref)
        kku = load(kku_ref)
        lr = load(a_ref)
        kk = kku / jnp.maximum(jnp.sqrt(head_sums(kku * kku)), 1e-12)
        cum = _cumsum_rows(tri_c, ld)
        last = [cum[(b + 1) * c - 1:(b + 1) * c, :] for b in range(nb)]
        total = jnp.concatenate([jnp.broadcast_to(x, (c, d)) for x in last], axis=0)
        b = kk * lr
        e_neg = jnp.exp(-cum)
        e_tot = jnp.exp(total - cum)
        at = -kk * jnp.exp(cum - ld)
        bt = b * e_neg
        kt = k * e_neg
        rt = r * jnp.exp(cum)
        bh = b * e_tot
        kh = k * e_tot
        dec = [jnp.exp(x) for x in last]
        items = [(b, g) for b in range(nb) for g in range(d // gw)]
        groups = range(len(items))
        pick = lambda x: [x[b * c:(b + 1) * c, g * gw:(g + 1) * gw] for b, g in items]
        stack = lambda top, bottom: jnp.concatenate([top, bottom], axis=0)
        blocks = lambda xs: [_block_rows(x, lo_c) for x in xs]
        at_g = pick(at)
        rt_g = pick(rt)
        v_g = pick(v)
        bh_g = pick(bh)
        kh_g = pick(kh)
        ar = [stack(at_g[g], rt_g[g]).astype(BF16) for g in groups]
        bt_b = blocks(pick(bt))
        kt_b = blocks(pick(kt))
        ab = [_dot(ar[g], bt_b[g], NT) for g in groups]
        ak = [_dot(ar[g], kt_b[g], NT) for g in groups]
        a_ab = [jnp.where(tri_strict, ab[g][:c], 0.0) for g in groups]
        m_rb = [jnp.where(tri_incl, ab[g][c:], 0.0) for g in groups]
        akmk = [stack(jnp.where(tri_strict, ak[g][:c], 0.0), jnp.where(tri_incl, ak[g][c:], 0.0)) for g in groups]
        inv = [eye + a_ab[g] for g in groups]
        pw_b = blocks(a_ab)
        power = [_dot(a_ab[g], pw_b[g]) for g in groups]
        for _ in range(4):
            pw_b = blocks(power)
            z = [_dot(stack(inv[g], power[g]), pw_b[g]) for g in groups]
            inv = [inv[g] + z[g][:c] for g in groups]
            power = [z[g][c:] for g in groups]
        pw_b = blocks(power)
        inv = [inv[g] + _dot(inv[g], pw_b[g]) for g in groups]
        v_b = blocks(v_g)
        z = [_dot(akmk[g], v_b[g]) for g in groups]
        akv = [z[g][:c] for g in groups]
        mrkv = [z[g][c:] for g in groups]
        at_b = blocks(at_g)
        akv_b = blocks(akv)
        a_hat = [_dot(inv[g], at_b[g]) for g in groups]
        u0 = [_dot(inv[g], akv_b[g]) for g in groups]
        ah_b = blocks(a_hat)
        u0_b = blocks(u0)
        r_hat = [rt_g[g] + _dot(m_rb[g], ah_b[g]) for g in groups]
        y0 = [_dot(m_rb[g], u0_b[g]) + mrkv[g] for g in groups]
        st = [st_ref[b, g] for b, g in items]
        z = [_dot(stack(a_hat[g], r_hat[g]), st[g]) for g in groups]
        u = [z[g][:c] + u0[g] for g in groups]
        yi = [z[g][c:] + y0[g] for g in groups]
        per_seq = d // gw
        y = jnp.concatenate([jnp.concatenate(yi[b * per_seq:(b + 1) * per_seq], axis=1) for b in range(nb)], axis=0)
        for i, (b, g) in enumerate(items):
            bk = stack(bh_g[i], kh_g[i])
            uv = stack(u[i], v_g[i])
            dec_col = jnp.transpose(jnp.broadcast_to(dec[b][:, g * gw:(g + 1) * gw], (LANE_TILE, gw)))
            st_ref[b, g] = (st[i] * jnp.concatenate([dec_col, dec_col], axis=1)
                            + jnp.where(same_head, _dot(bk, uv, TN), 0.0))
        yc = y - head_sums(y) * (1.0 / hd)
        var = head_sums(yc * yc) * (1.0 / hd)
        yn = yc * lax.rsqrt(var + GN_EPS) * lnw_ref[...] + lnb_ref[...]
        bonus = head_sums(r * k * rk_ref[...]) * v
        out = yn + bonus
        for b in range(nb):
            y_ref[b, rows, :] = out[b * c:(b + 1) * c]
        return carry

    lax.fori_loop(0, chunks, chunk_body, 0)


def _rwkv_scan(r, ld, k, v, kku, a, r_k, ln_w, ln_b, *, batch, seq, tb=256, nb=2):
    n, d = r.shape
    nb = nb if batch % nb == 0 else 1
    seqs = lambda x: x.reshape(batch, seq, d)
    blk = pl.BlockSpec((nb, tb, d), lambda b, t: (b, t, 0))
    vec = _resident((1, d))
    y = pl.pallas_call(
        functools.partial(_rwkv_scan_kernel, chunks=tb // CHUNK),
        grid=(batch // nb, seq // tb),
        in_specs=[blk] * 6 + [vec] * 3,
        out_specs=blk,
        out_shape=jax.ShapeDtypeStruct((batch, seq, d), F32),
        scratch_shapes=[pltpu.VMEM((nb, d // GROUP_W, GROUP_W, GROUP_W), F32)],
        compiler_params=_params(("parallel", "arbitrary")),
        name="rwkv_scan",
    )(seqs(r), seqs(ld), seqs(k), seqs(v), seqs(kku), seqs(a),
      r_k.reshape(1, d), ln_w.reshape(1, d), ln_b.reshape(1, d))
    return y.reshape(n, d)


def _gated_out_kernel(y_ref, gate_ref, h_ref, wo_ref, o_ref):
    o_ref[...] = h_ref[...] + jnp.dot((y_ref[...] * gate_ref[...]).astype(BF16), wo_ref[...],
                                      preferred_element_type=F32)


def _gated_out(y, gate, h, w_o, *, tm=512):
    n, d = h.shape
    row = pl.BlockSpec((tm, d), lambda i: (i, 0))
    return pl.pallas_call(
        _gated_out_kernel,
        grid=(n // tm,),
        in_specs=[row, row, row, _resident(w_o.shape)],
        out_specs=row,
        out_shape=jax.ShapeDtypeStruct((n, d), F32),
        compiler_params=_params(("parallel",)),
        name="rwkv_out",
    )(y, gate, h, w_o)


def kernel(x, norm_mix_g, norm_ffn_g, final_norm_g, ffn_w_in, ffn_w_out, attn_w_qkv, attn_w_o, attn_sinks, pool_w, pool_scale, rwkv_mu, rwkv_w_rkv, rwkv_w0, rwkv_w1, rwkv_w2, rwkv_a0, rwkv_a1, rwkv_a2, rwkv_g1, rwkv_g2, rwkv_k_k, rwkv_k_a, rwkv_r_k, rwkv_ln_w, rwkv_ln_b, rwkv_w_o):
    batch, seq, d = x.shape
    depth = norm_mix_g.shape[0]
    n_mixers = 3
    bf = lambda w: w.astype(BF16)
    h = x.reshape(batch * seq, d)
    for i in range(depth):
        kind, j = i % n_mixers, i // n_mixers
        g = norm_mix_g[i]
        if kind == 0:
            q, k, v = _qkv(h, g, bf(attn_w_qkv[j]))
            h = _attention(h, q, k, v, attn_sinks[j], bf(attn_w_o[j]), seq=seq)
        elif kind == 1:
            h = _pool(h, g, bf(pool_w[j]), pool_scale[j], seq=seq)
        else:
            r, ld, k, v, kku, a, gate = _rwkv_proj(
                h, g, rwkv_mu[j], bf(rwkv_w_rkv[j]), rwkv_w0[j], bf(rwkv_w1[j]), bf(rwkv_w2[j]), rwkv_a0[j],
                bf(rwkv_a1[j]), bf(rwkv_a2[j]), bf(rwkv_g1[j]), bf(rwkv_g2[j]), rwkv_k_k[j], rwkv_k_a[j], seq=seq)
            y = _rwkv_scan(r, ld, k, v, kku, a, rwkv_r_k[j], rwkv_ln_w[j], rwkv_ln_b[j], batch=batch, seq=seq)
            h = _gated_out(y, gate, h, bf(rwkv_w_o[j]))
        h = _ffn(h, norm_ffn_g[i], bf(ffn_w_in[i]), bf(ffn_w_out[i]), final_norm_g, final=(i == depth - 1))
    return h.reshape(batch, seq, d)
```

```python
import functools

import jax
import jax.numpy as jnp
from jax import lax
from jax.experimental import pallas as pl
from jax.experimental.pallas import tpu as pltpu

F32 = jnp.float32
BF16 = jnp.bfloat16

RMS_EPS = 1e-6
GN_EPS = 64e-5
ATTN_HEAD_DIM = 64
ATTN_Q_HEADS = 16
ATTN_KV_HEADS = 4
ATTN_GROUP = ATTN_Q_HEADS // ATTN_KV_HEADS
ATTN_BLOCK = 128
POOL_WINDOWS = (2, 4, 8, 16)
POOL_HALO = 16
RWKV_HEAD_DIM = 64
CHUNK = 64
LANE_TILE = 128

VMEM_LIMIT_BYTES = 56 * 1024 * 1024

NN = (((1,), (0,)), ((), ()))
NT = (((1,), (1,)), ((), ()))
TN = (((0,), (0,)), ((), ()))


def _dot(a, b, dims=NN):
    return lax.dot_general(a.astype(BF16), b.astype(BF16), dims, preferred_element_type=F32)


def _rms(x, g):
    return x * lax.rsqrt(jnp.mean(x * x, axis=-1, keepdims=True) + RMS_EPS) * g


def _params(sem):
    return pltpu.CompilerParams(dimension_semantics=sem, vmem_limit_bytes=VMEM_LIMIT_BYTES)


def _resident(shape):
    nd = len(shape)
    return pl.BlockSpec(shape, lambda *_: (0,) * nd, pipeline_mode=pl.Buffered(1))


def _ffn_kernel(*refs, hidden, tf, final, mix_dims):
    if mix_dims is None:
        h_ref, g_ref, win_ref, wout_ref, gf_ref, o_ref, act_ref = refs
        h = h_ref[...]
    else:
        a_ref, wo_ref, h_ref, g_ref, win_ref, wout_ref, gf_ref, o_ref, act_ref = refs
        h = h_ref[...] + lax.dot_general(a_ref[...], wo_ref[...], mix_dims, preferred_element_type=F32)
    xn = _rms(h, g_ref[...]).astype(BF16)
    for j in range(hidden // tf):
        gate = jnp.dot(xn, win_ref[:, j * tf:(j + 1) * tf], preferred_element_type=F32)
        up = jnp.dot(xn, win_ref[:, hidden + j * tf:hidden + (j + 1) * tf], preferred_element_type=F32)
        act_ref[:, j * tf:(j + 1) * tf] = (gate * jax.nn.sigmoid(gate) * up).astype(BF16)
    out = h + jnp.dot(act_ref[...], wout_ref[...], preferred_element_type=F32)
    if final:
        out = _rms(out, gf_ref[...])
    o_ref[...] = out


def _ffn(h, g, w_in, w_out, g_final, *, final, mix=None, tm=512, tf=256):
    n, d = h.shape
    hidden = w_out.shape[0]
    row = pl.BlockSpec((tm, d), lambda i: (i, 0))
    mix_args, mix_specs, mix_dims = [], [], None
    if mix is not None:
        act, w_o, transposed = mix
        mix_dims = TN if transposed else NN
        act_spec = (pl.BlockSpec((act.shape[0], tm), lambda i: (0, i)) if transposed
                    else pl.BlockSpec((tm, act.shape[1]), lambda i: (i, 0)))
        mix_args, mix_specs = [act, w_o], [act_spec, _resident(w_o.shape)]
    return pl.pallas_call(
        functools.partial(_ffn_kernel, hidden=hidden, tf=tf, final=final, mix_dims=mix_dims),
        grid=(n // tm,),
        in_specs=mix_specs + [row, _resident((1, d)), _resident(w_in.shape), _resident(w_out.shape),
                              _resident((1, d))],
        out_specs=row,
        out_shape=jax.ShapeDtypeStruct((n, d), F32),
        scratch_shapes=[pltpu.VMEM((tm, hidden), BF16)],
        compiler_params=_params(("parallel",)),
        name="ffn",
    )(*mix_args, h, g.reshape(1, d), w_in, w_out, g_final.reshape(1, d))


def _half_placed_tiles(x, lo, mask):
    tile = lambda a, t: a[:, t * LANE_TILE:(t + 1) * LANE_TILE]
    rolled = pltpu.roll(x, ATTN_HEAD_DIM, axis=1)
    out = []
    for g in range(ATTN_KV_HEADS):
        natural = tile(x, g // 2)
        moved = tile(rolled, ((g + 1) % ATTN_KV_HEADS) // 2)
        low, high = (natural, moved) if g % 2 == 0 else (moved, natural)
        if mask:
            low = jnp.where(lo, low, 0.0)
            high = jnp.where(lo, 0.0, high)
        out += [low, high]
    return out


def _qkv_kernel(h_ref, g_ref, w_ref, q_ref, kx_ref, vxt_ref, *, qd, kvd, scale):
    xn = _rms(h_ref[...], g_ref[...]).astype(BF16)
    qkv = jnp.dot(xn, w_ref[...], preferred_element_type=F32)
    q_ref[...] = (qkv[:, :qd] * scale).astype(BF16)
    lo = lax.broadcasted_iota(jnp.int32, (qkv.shape[0], LANE_TILE), 1) < ATTN_HEAD_DIM
    for i, t in enumerate(_half_placed_tiles(qkv[:, qd:qd + kvd], lo, True)):
        kx_ref[:, i * LANE_TILE:(i + 1) * LANE_TILE] = t.astype(BF16)
    for i, t in enumerate(_half_placed_tiles(qkv[:, qd + kvd:], lo, False)):
        vxt_ref[i * LANE_TILE:(i + 1) * LANE_TILE, :] = t.T.astype(BF16)


def _qkv(h, g, w_qkv, *, tm=512):
    n, d = h.shape
    qd = ATTN_Q_HEADS * ATTN_HEAD_DIM
    kvd = ATTN_KV_HEADS * ATTN_HEAD_DIM
    xd = 2 * ATTN_KV_HEADS * LANE_TILE
    return pl.pallas_call(
        functools.partial(_qkv_kernel, qd=qd, kvd=kvd, scale=ATTN_HEAD_DIM ** -0.5),
        grid=(n // tm,),
        in_specs=[pl.BlockSpec((tm, d), lambda i: (i, 0)), _resident((1, d)), _resident(w_qkv.shape)],
        out_specs=[pl.BlockSpec((tm, qd), lambda i: (i, 0)),
                   pl.BlockSpec((tm, xd), lambda i: (i, 0)),
                   pl.BlockSpec((xd, tm), lambda i: (0, i))],
        out_shape=[jax.ShapeDtypeStruct((n, qd), BF16),
                   jax.ShapeDtypeStruct((n, xd), BF16),
                   jax.ShapeDtypeStruct((xd, n), BF16)],
        compiler_params=_params(("parallel",)),
        name="attn_qkv",
    )(h, g.reshape(1, d), w_qkv)


def _attn_kernel(q_ref, kp_ref, kc_ref, vtp_ref, vtc_ref, sink_ref, oht_ref, *, blocks_per_seq):
    blk = ATTN_BLOCK
    hd = ATTN_HEAD_DIM
    first = (pl.program_id(0) % blocks_per_seq) == 0
    kpos = lax.broadcasted_iota(jnp.int32, (2 * blk, 2 * blk), 0)
    qpos = lax.broadcasted_iota(jnp.int32, (2 * blk, 2 * blk), 1) % blk + blk
    diff = qpos - kpos
    valid = (diff >= 0) & (diff < blk) & (jnp.logical_not(first) | (kpos >= blk))
    tile = lambda ref, t: ref[:, t * LANE_TILE:(t + 1) * LANE_TILE]
    kband = lambda t: jnp.concatenate([tile(kp_ref, t), tile(kc_ref, t)], axis=0)
    rows_t = lambda ref, t: ref[t * LANE_TILE:(t + 1) * LANE_TILE, :]
    vband_t = lambda t: jnp.concatenate([rows_t(vtp_ref, t), rows_t(vtc_ref, t)], axis=1)
    kv_heads = range(ATTN_KV_HEADS)
    qa = [jnp.concatenate([tile(q_ref, 2 * g), tile(q_ref, 2 * g + 1)], axis=0) for g in kv_heads]
    items = [(g, half) for g in kv_heads for half in range(2)]
    srow = lambda hq: jnp.broadcast_to(sink_ref[:, hq:hq + 1], (1, blk))
    sink = [jnp.concatenate([srow(4 * g + half), srow(4 * g + 2 + half)], axis=1) for g, half in items]
    s = [lax.dot_general(kband(2 * g + half), qa[g], NT, preferred_element_type=F32) for g, half in items]
    s = [jnp.where(valid, x, -jnp.inf) for x in s]
    m = [jnp.maximum(jnp.max(x, axis=0, keepdims=True), sk) for x, sk in zip(s, sink)]
    p = [jnp.exp(x - mx) for x, mx in zip(s, m)]
    den = [jnp.sum(x, axis=0, keepdims=True) + jnp.exp(sk - mx) for x, sk, mx in zip(p, sink, m)]
    o = [jnp.dot(vband_t(2 * g + half), x.astype(BF16), preferred_element_type=F32)
         for x, (g, half) in zip(p, items)]
    o = [x / dn for x, dn in zip(o, den)]
    for (g, half), x in zip(items, o):
        for j in range(2):
            hq = 4 * g + half + 2 * j
            oht_ref[hq * hd:(hq + 1) * hd, :] = x[half * hd:(half + 1) * hd, j * blk:(j + 1) * blk].astype(BF16)


def _attention(q, kx, vxt, sinks, *, seq):
    blk = ATTN_BLOCK
    n, qd = q.shape
    xd = kx.shape[1]
    bps = seq // blk
    cur = lambda i: (i, 0)
    prev = lambda i: (jnp.maximum(i - 1, 0), 0)
    cur_t = lambda i: (0, i)
    prev_t = lambda i: (0, jnp.maximum(i - 1, 0))
    return pl.pallas_call(
        functools.partial(_attn_kernel, blocks_per_seq=bps),
        grid=(n // blk,),
        in_specs=[pl.BlockSpec((blk, qd), cur),
                  pl.BlockSpec((blk, xd), prev), pl.BlockSpec((blk, xd), cur),
                  pl.BlockSpec((xd, blk), prev_t), pl.BlockSpec((xd, blk), cur_t),
                  _resident((1, ATTN_Q_HEADS))],
        out_specs=pl.BlockSpec((qd, blk), cur_t),
        out_shape=jax.ShapeDtypeStruct((qd, n), BF16),
        compiler_params=_params(("parallel",)),
        name="attn_core",
    )(q, kx, kx, vxt, vxt, sinks.reshape(1, -1))


def _pool_kernel(hp_ref, hc_ref, g_ref, pw_ref, ps_ref, o_ref, *, tiles_per_seq, tm):
    halo = POOL_HALO
    gdim = pw_ref.shape[1]
    tile = pl.program_id(0) % tiles_per_seq
    h = hc_ref[...]
    g = g_ref[...]
    keep = (tile > 0).astype(F32)
    ext = jnp.concatenate([_rms(hp_ref[...], g) * keep, _rms(h, g)], axis=0)
    pos = tile * tm + lax.broadcasted_iota(jnp.int32, (tm, 1), 0)
    for gi, w in enumerate(POOL_WINDOWS):
        csl = slice(gi * gdim, (gi + 1) * gdim)
        x = ext[:, csl]
        s = x
        span = 1
        while span < w:
            s = s + pltpu.roll(s, span, axis=0)
            span *= 2
        cnt = jnp.minimum(pos + 1, w).astype(F32)
        dpool = s[halo:, :] / cnt - x[halo:, :]
        y = jnp.dot(dpool.astype(BF16), pw_ref[gi], preferred_element_type=F32)
        o_ref[:, csl] = h[:, csl] + y * ps_ref[:, csl]


def _pool(h, g, pool_w, pool_scale, *, seq, tm=512):
    n, d = h.shape
    halo = POOL_HALO
    per = tm // halo
    return pl.pallas_call(
        functools.partial(_pool_kernel, tiles_per_seq=seq // tm, tm=tm),
        grid=(n // tm,),
        in_specs=[pl.BlockSpec((halo, d), lambda i: (jnp.maximum(i * per - 1, 0), 0)),
                  pl.BlockSpec((tm, d), lambda i: (i, 0)),
                  _resident((1, d)), _resident(pool_w.shape), _resident((1, d))],
        out_specs=pl.BlockSpec((tm, d), lambda i: (i, 0)),
        out_shape=jax.ShapeDtypeStruct((n, d), F32),
        compiler_params=_params(("parallel",)),
        name="pool_mix",
    )(h, h, g.reshape(1, d), pool_w, pool_scale.reshape(1, d))


def _rwkv_proj_kernel(hp_ref, hc_ref, g_ref, mu_ref, wrkv_ref, w0_ref, w1_ref, w2_ref, a0_ref, a1_ref, a2_ref,
                      g1_ref, g2_ref, kk_ref, ka_ref,
                      r_ref, ld_ref, k_ref, v_ref, kku_ref, a_ref, gate_ref, *, tiles_per_seq, tm):
    tile = pl.program_id(0) % tiles_per_seq
    g = g_ref[...]
    xn = _rms(hc_ref[...], g)
    last = _rms(hp_ref[...], g)[-1:, :] * (tile > 0).astype(F32)
    rowid = lax.broadcasted_iota(jnp.int32, (tm, 1), 0)
    shifted = jnp.where(rowid == 0, last, pltpu.roll(xn, 1, axis=0))
    dx = shifted - xn

    def mix(c):
        return (xn + dx * mu_ref[c:c + 1, :]).astype(BF16)

    r = jnp.dot(mix(0), wrkv_ref[0], preferred_element_type=F32)
    k = jnp.dot(mix(1), wrkv_ref[1], preferred_element_type=F32)
    v = jnp.dot(mix(2), wrkv_ref[2], preferred_element_type=F32)
    wl = w0_ref[...] + _dot(jnp.tanh(jnp.dot(mix(3), w1_ref[...], preferred_element_type=F32)), w2_ref[...])
    w_log = -jax.nn.softplus(-wl) - 0.5
    a = jax.nn.sigmoid(a0_ref[...] + _dot(jnp.dot(mix(4), a1_ref[...], preferred_element_type=F32), a2_ref[...]))
    gate = _dot(jax.nn.sigmoid(jnp.dot(mix(5), g1_ref[...], preferred_element_type=F32)), g2_ref[...])
    r_ref[...] = r
    ld_ref[...] = -jnp.exp(w_log)
    k_ref[...] = k * (1.0 + (a - 1.0) * ka_ref[...])
    v_ref[...] = v
    kku_ref[...] = k * kk_ref[...]
    a_ref[...] = a
    gate_ref[...] = gate


def _rwkv_proj(h, g, mu, w_rkv, w0, w1, w2, a0, a1, a2, g1, g2, k_k, k_a, *, seq, tm=512):
    n, d = h.shape
    row = pl.BlockSpec((tm, d), lambda i: (i, 0))
    per = tm // 8
    vec = lambda x: x.reshape(1, d)
    outs = [jax.ShapeDtypeStruct((n, d), F32)] * 7
    return pl.pallas_call(
        functools.partial(_rwkv_proj_kernel, tiles_per_seq=seq // tm, tm=tm),
        grid=(n // tm,),
        in_specs=[pl.BlockSpec((8, d), lambda i: (jnp.maximum(i * per - 1, 0), 0)), row,
                  _resident((1, d)), _resident(mu.shape), _resident(w_rkv.shape),
                  _resident((1, d)), _resident(w1.shape), _resident(w2.shape),
                  _resident((1, d)), _resident(a1.shape), _resident(a2.shape),
                  _resident(g1.shape), _resident(g2.shape), _resident((1, d)), _resident((1, d))],
        out_specs=[row] * 7,
        out_shape=outs,
        compiler_params=_params(("parallel",)),
        name="rwkv_proj",
    )(h, h, vec(g), mu, w_rkv, vec(w0), w1, w2, vec(a0), a1, a2, g1, g2, vec(k_k), vec(k_a))


HEAD_GROUP = 4
GROUP_W = HEAD_GROUP * RWKV_HEAD_DIM
assert CHUNK == RWKV_HEAD_DIM and GROUP_W == 2 * LANE_TILE


def _block_rows(x, lo):
    x0, x1 = x[:, :LANE_TILE], x[:, LANE_TILE:]
    z = jnp.zeros_like(x0)
    keep = lambda m, t: jnp.where(m, t, z)
    hi = jnp.logical_not(lo)
    return jnp.concatenate([
        jnp.concatenate([keep(lo, x0), z], axis=1),
        jnp.concatenate([keep(hi, x0), z], axis=1),
        jnp.concatenate([z, keep(lo, x1)], axis=1),
        jnp.concatenate([z, keep(hi, x1)], axis=1)], axis=0)


def _cumsum_rows(tri_incl_bf16, x):
    hi = x.astype(BF16)
    rem = x - hi.astype(F32)
    mid = rem.astype(BF16)
    lo = (rem - mid.astype(F32)).astype(BF16)
    f = lambda z: jnp.dot(tri_incl_bf16, z, preferred_element_type=F32)
    return f(hi) + f(mid) + f(lo)


def _rwkv_scan_kernel(r_ref, ld_ref, k_ref, v_ref, kku_ref, a_ref, gate_ref, rk_ref, lnw_ref, lnb_ref, y_ref,
                      st_ref, *, chunks):
    hd = RWKV_HEAD_DIM
    c = CHUNK
    gw = GROUP_W
    nb, _, d = r_ref.shape

    @pl.when(pl.program_id(1) == 0)
    def _():
        st_ref[...] = jnp.zeros_like(st_ref)

    t_idx = lax.broadcasted_iota(jnp.int32, (c, gw), 0)
    s_idx = lax.broadcasted_iota(jnp.int32, (c, gw), 1) % c
    tri_strict = t_idx > s_idx
    tri_incl = t_idx >= s_idx
    eye = (t_idx == s_idx).astype(F32)
    lo = lax.broadcasted_iota(jnp.int32, (nb * c, LANE_TILE), 1) < hd
    lo_c = lax.broadcasted_iota(jnp.int32, (c, LANE_TILE), 1) < hd
    same_head = (lax.broadcasted_iota(jnp.int32, (gw, gw), 0) // hd
                 == lax.broadcasted_iota(jnp.int32, (gw, gw), 1) // hd)
    row_c = lax.broadcasted_iota(jnp.int32, (nb * c, nb * c), 0)
    col_c = lax.broadcasted_iota(jnp.int32, (nb * c, nb * c), 1)
    tri_c = ((row_c >= col_c) & (row_c // c == col_c // c)).astype(BF16)

    def head_sums(x):
        tiles = []
        for t in range(d // LANE_TILE):
            xt = x[:, t * LANE_TILE:(t + 1) * LANE_TILE]
            s0 = jnp.sum(jnp.where(lo, xt, 0.0), axis=-1, keepdims=True)
            s1 = jnp.sum(jnp.where(lo, 0.0, xt), axis=-1, keepdims=True)
            tiles.append(jnp.where(lo, s0, s1))
        return jnp.concatenate(tiles, axis=1)

    def chunk_body(ci, carry):
        rows = pl.ds(pl.multiple_of(ci * c, c), c)
        load = lambda ref: jnp.concatenate([ref[b, rows, :] for b in range(nb)], axis=0)
        r = load(r_ref)
        ld = load(ld_ref)
        k = load(k_ref)
        v = load(v_ref)
        kku = load(kku_ref)
        lr = load(a_ref)
        kk = kku / jnp.maximum(jnp.sqrt(head_sums(kku * kku)), 1e-12)
        cum = _cumsum_rows(tri_c, ld)
        last = [cum[(b + 1) * c - 1:(b + 1) * c, :] for b in range(nb)]
        total = jnp.concatenate([jnp.broadcast_to(x, (c, d)) for x in last], axis=0)
        b = kk * lr
        e_neg = jnp.exp(-cum)
        e_tot = jnp.exp(total - cum)
        at = -kk * jnp.exp(cum - ld)
        bt = b * e_neg
        kt = k * e_neg
        rt = r * jnp.exp(cum)
        bh = b * e_tot
        kh = k * e_tot
        dec = [jnp.exp(x) for x in last]
        items = [(b, g) for b in range(nb) for g in range(d // gw)]
        groups = range(len(items))
        pick = lambda x: [x[b * c:(b + 1) * c, g * gw:(g + 1) * gw] for b, g in items]
        stack = lambda top, bottom: jnp.concatenate([top, bottom], axis=0)
        blocks = lambda xs: [_block_rows(x, lo_c) for x in xs]
        bf = lambda xs: [x.astype(BF16) for x in xs]
        at_g = pick(at.astype(BF16))
        rt_g = pick(rt)
        v_g = pick(v.astype(BF16))
        bk = [stack(x, y) for x, y in zip(pick(bh.astype(BF16)), pick(kh.astype(BF16)))]
        ar = [stack(at_g[g], rt_g[g].astype(BF16)) for g in groups]
        bt_b = blocks(pick(bt.astype(BF16)))
        kt_b = blocks(pick(kt.astype(BF16)))
        ab = [_dot(ar[g], bt_b[g], NT) for g in groups]
        ak = [_dot(ar[g], kt_b[g], NT) for g in groups]
        a_ab = [jnp.where(tri_strict, ab[g][:c], 0.0) for g in groups]
        m_rb = bf([jnp.where(tri_incl, ab[g][c:], 0.0) for g in groups])
        akmk = bf([stack(jnp.where(tri_strict, ak[g][:c], 0.0), jnp.where(tri_incl, ak[g][c:], 0.0))
                   for g in groups])
        inv = [eye + a_ab[g] for g in groups]
        power = bf(a_ab)
        pw_b = blocks(power)
        power = bf([_dot(power[g], pw_b[g]) for g in groups])
        for _ in range(4):
            pw_b = blocks(power)
            z = [_dot(stack(inv[g].astype(BF16), power[g]), pw_b[g]) for g in groups]
            inv = [inv[g] + z[g][:c] for g in groups]
            power = bf([z[g][c:] for g in groups])
        pw_b = blocks(power)
        inv = bf([inv[g] + _dot(inv[g], pw_b[g]) for g in groups])
        v_b = blocks(v_g)
        z = [_dot(akmk[g], v_b[g]) for g in groups]
        akv_b = blocks(bf([z[g][:c] for g in groups]))
        mrkv = [z[g][c:] for g in groups]
        at_b = blocks(at_g)
        a_hat = bf([_dot(inv[g], at_b[g]) for g in groups])
        u0 = [_dot(inv[g], akv_b[g]) for g in groups]
        ah_b = blocks(a_hat)
        u0_b = blocks(bf(u0))
        r_hat = bf([rt_g[g] + _dot(m_rb[g], ah_b[g]) for g in groups])
        y0 = [_dot(m_rb[g], u0_b[g]) + mrkv[g] for g in groups]
        st = [st_ref[b, g] for b, g in items]
        z = [_dot(stack(a_hat[g], r_hat[g]), st[g]) for g in groups]
        u = [z[g][:c] + u0[g] for g in groups]
        yi = [z[g][c:] + y0[g] for g in groups]
        per_seq = d // gw
        y = jnp.concatenate([jnp.concatenate(yi[b * per_seq:(b + 1) * per_seq], axis=1) for b in range(nb)], axis=0)
        for i, (b, g) in enumerate(items):
            uv = stack(u[i].astype(BF16), v_g[i])
            dec_col = jnp.transpose(jnp.broadcast_to(dec[b][:, g * gw:(g + 1) * gw], (LANE_TILE, gw)))
            st_ref[b, g] = (st[i] * jnp.concatenate([dec_col, dec_col], axis=1)
                            + jnp.where(same_head, _dot(bk[i], uv, TN), 0.0))
        yc = y - head_sums(y) * (1.0 / hd)
        var = head_sums(yc * yc) * (1.0 / hd)
        yn = yc * lax.rsqrt(var + GN_EPS) * lnw_ref[...] + lnb_ref[...]
        bonus = head_sums(r * k * rk_ref[...]) * v
        out = ((yn + bonus) * load(gate_ref)).astype(BF16)
        for b in range(nb):
            y_ref[b, rows, :] = out[b * c:(b + 1) * c]
        return carry

    lax.fori_loop(0, chunks, chunk_body, 0)


def _rwkv_scan(r, ld, k, v, kku, a, gate, r_k, ln_w, ln_b, *, batch, seq, tb=256, nb=2):
    n, d = r.shape
    nb = nb if batch % nb == 0 else 1
    seqs = lambda x: x.reshape(batch, seq, d)
    blk = pl.BlockSpec((nb, tb, d), lambda b, t: (b, t, 0))
    vec = _resident((1, d))
    y = pl.pallas_call(
        functools.partial(_rwkv_scan_kernel, chunks=tb // CHUNK),
        grid=(batch // nb, seq // tb),
        in_specs=[blk] * 7 + [vec] * 3,
        out_specs=blk,
        out_shape=jax.ShapeDtypeStruct((batch, seq, d), BF16),
        scratch_shapes=[pltpu.VMEM((nb, d // GROUP_W, GROUP_W, GROUP_W), F32)],
        compiler_params=_params(("parallel", "arbitrary")),
        name="rwkv_scan",
    )(seqs(r), seqs(ld), seqs(k), seqs(v), seqs(kku), seqs(a), seqs(gate),
      r_k.reshape(1, d), ln_w.reshape(1, d), ln_b.reshape(1, d))
    return y.reshape(n, d)


def kernel(x, norm_mix_g, norm_ffn_g, final_norm_g, ffn_w_in, ffn_w_out, attn_w_qkv, attn_w_o, attn_sinks, pool_w, pool_scale, rwkv_mu, rwkv_w_rkv, rwkv_w0, rwkv_w1, rwkv_w2, rwkv_a0, rwkv_a1, rwkv_a2, rwkv_g1, rwkv_g2, rwkv_k_k, rwkv_k_a, rwkv_r_k, rwkv_ln_w, rwkv_ln_b, rwkv_w_o):
    batch, seq, d = x.shape
    depth = norm_mix_g.shape[0]
    n_mixers = 3
    bf = lambda w: w.astype(BF16)
    h = x.reshape(batch * seq, d)
    for i in range(depth):
        kind, j = i % n_mixers, i // n_mixers
        g = norm_mix_g[i]
        mix = None
        if kind == 0:
            q, kx, vxt = _qkv(h, g, bf(attn_w_qkv[j]))
            mix = (_attention(q, kx, vxt, attn_sinks[j], seq=seq), bf(attn_w_o[j]), True)
        elif kind == 1:
            h = _pool(h, g, bf(pool_w[j]), pool_scale[j], seq=seq)
        else:
            r, ld, k, v, kku, a, gate = _rwkv_proj(
                h, g, rwkv_mu[j], bf(rwkv_w_rkv[j]), rwkv_w0[j], bf(rwkv_w1[j]), bf(rwkv_w2[j]), rwkv_a0[j],
                bf(rwkv_a1[j]), bf(rwkv_a2[j]), bf(rwkv_g1[j]), bf(rwkv_g2[j]), rwkv_k_k[j], rwkv_k_a[j], seq=seq)
            y = _rwkv_scan(r, ld, k, v, kku, a, gate, rwkv_r_k[j], rwkv_ln_w[j], rwkv_ln_b[j],
                           batch=batch, seq=seq)
            mix = (y, bf(rwkv_w_o[j]), False)
        h = _ffn(h, norm_ffn_g[i], bf(ffn_w_in[i]), bf(ffn_w_out[i]), final_norm_g, final=(i == depth - 1),
                 mix=mix)
    return h.reshape(batch, seq, d)
```
